```python
import jax, jax.numpy as jnp
from jax import lax
import numpy as np

D_MODEL = 1024
BATCH = 16
SEQ = 2048
DEPTH = 2

CHUNK = 64
RET_HEADS = 4
RET_QK_DIM = 128
RET_V_DIM = 256
RET_QK = RET_HEADS * RET_QK_DIM
RET_V = RET_HEADS * RET_V_DIM
SC_WIDTH = D_MODEL
SC_KERNEL = 3
CF_WIDTH = D_MODEL
CF_KERNEL = 31
N_BRANCH = 3
D_FF = 4 * D_MODEL
ROPE_BASE = 10000.0
NORM_EPS = 1e-6
LN_EPS = 1e-5
N_NORMS = 6
IN_SPLITS = (RET_QK, RET_QK, RET_V, RET_V, SC_WIDTH, SC_WIDTH, SC_WIDTH, 2 * CF_WIDTH, N_BRANCH * D_MODEL)
D_IN = 2 * RET_QK + 2 * RET_V + 3 * SC_WIDTH + 2 * CF_WIDTH + N_BRANCH * D_MODEL

kernel_name = "hybrid_retention_conv_macaron_encoder"


def rms_norm(x, g):
    xf = x.astype(jnp.float32)
    y = xf * lax.rsqrt(jnp.mean(xf * xf, axis=-1, keepdims=True) + NORM_EPS)
    return (y * g.astype(jnp.float32)).astype(x.dtype)


def layer_norm(x, g, b):
    xf = x.astype(jnp.float32)
    mu = jnp.mean(xf, axis=-1, keepdims=True)
    var = jnp.mean(jnp.square(xf - mu), axis=-1, keepdims=True)
    y = (xf - mu) * lax.rsqrt(var + LN_EPS)
    return (y * g.astype(jnp.float32) + b.astype(jnp.float32)).astype(x.dtype)


def swiglu_ffn(h, w_gu, w_down):
    gate, up = jnp.split(h @ w_gu, 2, axis=-1)
    return (jax.nn.silu(gate) * up) @ w_down


def causal_depthwise_conv(x, w):
    k = w.shape[0]
    return lax.conv_general_dilated(
        x, w[:, None, :].astype(x.dtype), window_strides=(1,), padding=[(k - 1, 0)],
        dimension_numbers=("NWC", "WIO", "NWC"), feature_group_count=x.shape[-1])


def rotary(x, positions):
    half = x.shape[-1] // 2
    inv_freq = ROPE_BASE ** (-jnp.arange(half, dtype=jnp.float32) / half)
    ang = positions.astype(jnp.float32)[..., None] * inv_freq
    cos = jnp.cos(ang)[:, :, None, :]
    sin = jnp.sin(ang)[:, :, None, :]
    x1, x2 = x[..., :half], x[..., half:]
    return jnp.concatenate([x1 * cos - x2 * sin, x2 * cos + x1 * sin], axis=-1)


def chunkwise_retention(q, k, v, positions):
    b, s, h, dk = q.shape
    dv = v.shape[-1]
    n = s // CHUNK
    q = rotary(q, positions)
    k = rotary(k, positions) * (dk ** -0.5)
    log_g = jnp.log(1.0 - 2.0 ** (-5.0 - jnp.arange(h, dtype=jnp.float32)))
    idx = jnp.arange(CHUNK, dtype=jnp.float32)
    decay_intra = jnp.exp(log_g[:, None, None] * jnp.abs(idx[:, None] - idx[None, :]))
    xi = jnp.exp(log_g[None, :] * (idx[:, None] + 1.0))
    zeta = jnp.exp(log_g[None, :] * (CHUNK - 1.0 - idx[:, None]))
    g_chunk = jnp.exp(log_g * CHUNK)

    qc = q.reshape(b, n, CHUNK, h, dk)
    kc = k.reshape(b, n, CHUNK, h, dk)
    vc = v.reshape(b, n, CHUNK, h, dv)
    scores = jnp.einsum("bnihd,bnjhd->bnhij", qc, kc) * decay_intra
    intra = jnp.einsum("bnhij,bnjhe->bnihe", scores, vc)

    kz = kc * zeta[None, None, :, :, None]

    def step(state, inp):
        q_n, k_n, v_n = inp
        inter = jnp.einsum("bihd,bhde->bihe", q_n, state) * xi[None, :, :, None]
        state = state * g_chunk[None, :, None, None] + jnp.einsum("bjhd,bjhe->bhde", k_n, v_n)
        return state, inter

    state0 = jnp.zeros((b, h, dk, dv), jnp.float32)
    _, inter = lax.scan(step, state0, (qc.swapaxes(0, 1), kz.swapaxes(0, 1), vc.swapaxes(0, 1)))
    out = intra + inter.swapaxes(0, 1)
    return out.reshape(b, s, h, dv)


def head_norm(o):
    mu = jnp.mean(o, axis=-1, keepdims=True)
    var = jnp.mean(jnp.square(o - mu), axis=-1, keepdims=True)
    return (o - mu) * lax.rsqrt(var + LN_EPS)


def hybrid_mixer(h, positions, w_in, w_ret_o, sc_conv_w, w_sc_o,
                 cf_dw_w, cf_dw_b, cf_ln_g, cf_ln_b, w_cf_o, w_o):
    b, s, _ = h.shape
    split_at = [int(v) for v in np.cumsum(IN_SPLITS)[:-1]]
    q, k, v, g_ret, sc_b, sc_c, sc_x, cf_in, gate_logits = jnp.split(h @ w_in, split_at, axis=-1)

    o = chunkwise_retention(
        q.reshape(b, s, RET_HEADS, RET_QK_DIM).astype(jnp.float32),
        k.reshape(b, s, RET_HEADS, RET_QK_DIM).astype(jnp.float32),
        v.reshape(b, s, RET_HEADS, RET_V_DIM).astype(jnp.float32),
        positions)
    o = head_norm(o).reshape(b, s, RET_V).astype(h.dtype)
    y_a = (jax.nn.silu(g_ret) * o) @ w_ret_o

    y_b = (sc_b * causal_depthwise_conv(sc_c * sc_x, sc_conv_w)) @ w_sc_o

    glu_a, glu_b = jnp.split(cf_in, 2, axis=-1)
    u = glu_a * jax.nn.sigmoid(glu_b)
    u = causal_depthwise_conv(u, cf_dw_w) + cf_dw_b
    u = jax.nn.silu(layer_norm(u, cf_ln_g, cf_ln_b))
    y_c = u @ w_cf_o

    gates = jax.nn.sigmoid(gate_logits).reshape(b, s, N_BRANCH, D_MODEL)
    merged = gates[:, :, 0] * y_a + gates[:, :, 1] * y_b + gates[:, :, 2] * y_c
    return merged @ w_o


def setup_inputs(seed: int = 0) -> dict:
    key = jax.random.key(seed)
    ks = jax.random.split(key, 20)
    f32 = jnp.float32

    def dense(k, shape, fan_in):
        return jax.random.normal(k, shape, f32) * (fan_in ** -0.5)

    x = jax.random.normal(ks[0], (BATCH, SEQ, D_MODEL), f32)
    offsets = jax.random.randint(ks[1], (BATCH, 1), 0, 64, dtype=jnp.int32) * CHUNK
    positions = offsets + jnp.arange(SEQ, dtype=jnp.int32)[None, :]
    return {
        "x": x,
        "positions": positions,
        "norm_g": 1.0 + 0.02 * jax.random.normal(ks[2], (DEPTH, N_NORMS, D_MODEL), f32),
        "ffn1_w_gu": dense(ks[3], (DEPTH, D_MODEL, 2 * D_FF), D_MODEL),
        "ffn1_w_down": dense(ks[4], (DEPTH, D_FF, D_MODEL), D_FF),
        "w_in": dense(ks[5], (DEPTH, D_MODEL, D_IN), D_MODEL),
        "w_ret_o": dense(ks[6], (DEPTH, RET_V, D_MODEL), RET_V),
        "sc_conv_w": dense(ks[7], (DEPTH, SC_KERNEL, SC_WIDTH), SC_KERNEL),
        "w_sc_o": dense(ks[8], (DEPTH, SC_WIDTH, D_MODEL), SC_WIDTH),
        "cf_dw_w": dense(ks[9], (DEPTH, CF_KERNEL, CF_WIDTH), CF_KERNEL),
        "cf_dw_b": 0.02 * jax.random.normal(ks[10], (DEPTH, CF_WIDTH), f32),
        "cf_ln_g": 1.0 + 0.02 * jax.random.normal(ks[11], (DEPTH, CF_WIDTH), f32),
        "cf_ln_b": 0.02 * jax.random.normal(ks[12], (DEPTH, CF_WIDTH), f32),
        "w_cf_o": dense(ks[13], (DEPTH, CF_WIDTH, D_MODEL), CF_WIDTH),
        "w_o": dense(ks[14], (DEPTH, D_MODEL, D_MODEL), D_MODEL),
        "ffn2_w_gu": dense(ks[15], (DEPTH, D_MODEL, 2 * D_FF), D_MODEL),
        "ffn2_w_down": dense(ks[16], (DEPTH, D_FF, D_MODEL), D_FF),
    }


def reference(x, positions, norm_g, ffn1_w_gu, ffn1_w_down, w_in, w_ret_o, sc_conv_w, w_sc_o,
              cf_dw_w, cf_dw_b, cf_ln_g, cf_ln_b, w_cf_o, w_o, ffn2_w_gu, ffn2_w_down):
    for l in range(DEPTH):
        g = norm_g[l]
        x = x + 0.5 * rms_norm(swiglu_ffn(rms_norm(x, g[0]), ffn1_w_gu[l], ffn1_w_down[l]), g[1])
        m = hybrid_mixer(rms_norm(x, g[2]), positions, w_in[l], w_ret_o[l], sc_conv_w[l], w_sc_o[l],
                         cf_dw_w[l], cf_dw_b[l], cf_ln_g[l], cf_ln_b[l], w_cf_o[l], w_o[l])
        x = x + rms_norm(m, g[3])
        x = x + 0.5 * rms_norm(swiglu_ffn(rms_norm(x, g[4]), ffn2_w_gu[l], ffn2_w_down[l]), g[5])
    return x
```

```python
import functools

import jax
import jax.numpy as jnp
from jax import lax
from jax.experimental import pallas as pl
from jax.experimental.pallas import tpu as pltpu

f32 = jnp.float32
bf16 = jnp.bfloat16

D_MODEL = 1024
DEPTH = 2
CHUNK = 64
RET_HEADS = 4
RET_QK_DIM = 128
RET_V_DIM = 256
RET_QK = RET_HEADS * RET_QK_DIM
RET_V = RET_HEADS * RET_V_DIM
SC_KERNEL = 3
CF_KERNEL = 31
D_FF = 4 * D_MODEL
ROPE_BASE = 10000.0
NORM_EPS = 1e-6
LN_EPS = 1e-5

COL_Q = 0
COL_K = COL_Q + RET_QK
COL_V = COL_K + RET_QK
COL_G = COL_V + RET_V
COL_SCB = COL_G + RET_V
COL_SCC = COL_SCB + D_MODEL
COL_SCX = COL_SCC + D_MODEL
COL_CFA = COL_SCX + D_MODEL
COL_CFB = COL_CFA + D_MODEL
COL_GATE = COL_CFB + D_MODEL
D_IN = COL_GATE + 3 * D_MODEL

V7X_VMEM_LIMIT_BYTES = 56 * 1024 * 1024
FFN_TM = 1024
FFN_TF = 512
MIX_TS = 256
HALO = 32
CONV_ROWS = 64
CONV_LANES = 256
ROPE_TS = 512

_NT = (((1,), (1,)), ((), ()))
_TN = (((0,), (0,)), ((), ()))


def _rms(x, g):
    ms = jnp.mean(x * x, axis=-1, keepdims=True)
    return (x * lax.rsqrt(ms + NORM_EPS)) * g


def _silu(x):
    return x * jax.nn.sigmoid(x)


def _rope_body(pos_ref, inv_ref, sign_ref, cos_ref, sin_ref):
    ang = pos_ref[...].astype(f32) * inv_ref[...]
    cos_ref[...] = jnp.cos(ang)
    sin_ref[...] = jnp.sin(ang) * sign_ref[...]


def _rope_tables(positions):
    b, s = positions.shape
    half = RET_QK_DIM // 2
    inv = ROPE_BASE ** (-jnp.arange(half, dtype=f32) / half)
    inv2 = jnp.concatenate([inv, inv])[None, :]
    sign = jnp.concatenate([-jnp.ones((half,), f32), jnp.ones((half,), f32)])[None, :]
    out = jax.ShapeDtypeStruct((b, s, RET_QK_DIM), f32)
    return pl.pallas_call(
        _rope_body,
        grid=(b, s // ROPE_TS),
        in_specs=[pl.BlockSpec((None, ROPE_TS, 1), lambda i, j: (i, j, 0)),
                  pl.BlockSpec((1, RET_QK_DIM), lambda i, j: (0, 0)),
                  pl.BlockSpec((1, RET_QK_DIM), lambda i, j: (0, 0))],
        out_specs=[pl.BlockSpec((None, ROPE_TS, RET_QK_DIM), lambda i, j: (i, j, 0))] * 2,
        out_shape=[out, out],
        name="rope_tables",
    )(positions[:, :, None], inv2, sign)


def _ffn_body(x_ref, gpre_ref, gpost_ref, wg_ref, wu_ref, wd_ref, o_ref, h_s, acc_s):
    j = pl.program_id(1)

    @pl.when(j == 0)
    def _():
        h_s[...] = _rms(x_ref[...], gpre_ref[...]).astype(bf16)

    h = h_s[...]
    g = jnp.dot(h, wg_ref[...], preferred_element_type=f32)
    u = jnp.dot(h, wu_ref[...], preferred_element_type=f32)
    a = (_silu(g) * u).astype(bf16)
    p = jnp.dot(a, wd_ref[...], preferred_element_type=f32)

    @pl.when(j == 0)
    def _():
        acc_s[...] = p

    @pl.when(j > 0)
    def _():
        acc_s[...] += p

    @pl.when(j == pl.num_programs(1) - 1)
    def _():
        o_ref[...] = x_ref[...] + 0.5 * _rms(acc_s[...], gpost_ref[...])


def _ffn(x2, g_pre, g_post, w_gu, w_down):
    t, d = x2.shape
    nf = D_FF // FFN_TF
    return pl.pallas_call(
        _ffn_body,
        grid=(t // FFN_TM, nf),
        in_specs=[pl.BlockSpec((FFN_TM, d), lambda i, j: (i, 0)),
                  pl.BlockSpec((1, d), lambda i, j: (0, 0)),
                  pl.BlockSpec((1, d), lambda i, j: (0, 0)),
                  pl.BlockSpec((d, FFN_TF), lambda i, j: (0, j)),
                  pl.BlockSpec((d, FFN_TF), lambda i, j: (0, j + nf)),
                  pl.BlockSpec((FFN_TF, d), lambda i, j: (j, 0))],
        out_specs=pl.BlockSpec((FFN_TM, d), lambda i, j: (i, 0)),
        out_shape=jax.ShapeDtypeStruct((t, d), f32),
        scratch_shapes=[pltpu.VMEM((FFN_TM, d), bf16), pltpu.VMEM((FFN_TM, d), f32)],
        compiler_params=pltpu.CompilerParams(
            dimension_semantics=("arbitrary", "arbitrary"),
            vmem_limit_bytes=V7X_VMEM_LIMIT_BYTES),
        name="ffn",
    )(x2, g_pre[None, :], g_post[None, :], w_gu, w_gu, w_down)


def _causal_dwconv(buf_ref, w_ref, ksize, out_ref):
    base = HALO - (ksize - 1)
    for c0 in range(0, D_MODEL, CONV_LANES):
        cs = slice(c0, c0 + CONV_LANES)
        for r0 in range(0, MIX_TS, CONV_ROWS):
            acc = w_ref[0:1, cs] * buf_ref[pl.ds(base + r0, CONV_ROWS), cs]
            for k in range(1, ksize):
                acc = acc + w_ref[k:k + 1, cs] * buf_ref[pl.ds(base + r0 + k, CONV_ROWS), cs]
            out_ref[r0:r0 + CONV_ROWS, cs] = acc


def _mixer_body(x_ref, cos_ref, sin_ref, gpre_ref, gpost_ref, win_ref, wro_ref, wso_ref, wco_ref, wo_ref,
                scw_ref, cfw_ref, cfb_ref, lng_ref, lnb_ref, dmask_ref, xi_ref, zeta_ref, gl_ref,
                o_ref, h_s, o_s, state_s, zbuf_s, ubuf_s, conv_s, m_s):
    t = pl.program_id(1)

    @pl.when(t == 0)
    def _():
        state_s[...] = jnp.zeros_like(state_s)
        zbuf_s[0:HALO, :] = jnp.zeros((HALO, D_MODEL), f32)
        ubuf_s[0:HALO, :] = jnp.zeros((HALO, D_MODEL), f32)

    h_s[...] = _rms(x_ref[...], gpre_ref[...]).astype(bf16)

    def proj(col, width):
        return jnp.dot(h_s[...], win_ref[:, col:col + width], preferred_element_type=f32)

    cos = cos_ref[...]
    sin = sin_ref[...]
    qk = proj(COL_Q, 2 * RET_QK)
    v = proj(COL_V, RET_V).astype(bf16)
    for hd in range(RET_HEADS):
        qh = qk[:, hd * RET_QK_DIM:(hd + 1) * RET_QK_DIM]
        kh = qk[:, RET_QK + hd * RET_QK_DIM:RET_QK + (hd + 1) * RET_QK_DIM]
        qr = qh * cos + pltpu.roll(qh, RET_QK_DIM // 2, axis=1) * sin
        kr = (kh * cos + pltpu.roll(kh, RET_QK_DIM // 2, axis=1) * sin) * (RET_QK_DIM ** -0.5)
        qb = qr.astype(bf16)
        kb = kr.astype(bf16)
        kzb = (kr * zeta_ref[hd]).astype(bf16)
        vh = v[:, hd * RET_V_DIM:(hd + 1) * RET_V_DIM]
        scores = lax.dot_general(qb, kb, _NT, preferred_element_type=f32) * dmask_ref[hd]
        state = state_s[hd]
        o = (jnp.dot(scores.astype(bf16), vh, preferred_element_type=f32)
             + jnp.dot(qb, state.astype(bf16), preferred_element_type=f32) * xi_ref[hd])
        state_s[hd] = state * gl_ref[hd] + lax.dot_general(kzb, vh, _TN, preferred_element_type=f32)
        mu = jnp.mean(o, axis=-1, keepdims=True)
        dev = o - mu
        var = jnp.mean(dev * dev, axis=-1, keepdims=True)
        on = dev * lax.rsqrt(var + LN_EPS)
        g = proj(COL_G + hd * RET_V_DIM, RET_V_DIM)
        o_s[:, hd * RET_V_DIM:(hd + 1) * RET_V_DIM] = (_silu(g) * on).astype(bf16)
    y_a = jnp.dot(o_s[...], wro_ref[...], preferred_element_type=f32)
    m_s[...] = jax.nn.sigmoid(proj(COL_GATE, D_MODEL)) * y_a

    zbuf_s[HALO:HALO + MIX_TS, :] = proj(COL_SCC, D_MODEL) * proj(COL_SCX, D_MODEL)
    _causal_dwconv(zbuf_s, scw_ref, SC_KERNEL, conv_s)
    zbuf_s[0:HALO, :] = zbuf_s[MIX_TS:MIX_TS + HALO, :]
    yb_in = (proj(COL_SCB, D_MODEL) * conv_s[...]).astype(bf16)
    y_b = jnp.dot(yb_in, wso_ref[...], preferred_element_type=f32)
    m_s[...] += jax.nn.sigmoid(proj(COL_GATE + D_MODEL, D_MODEL)) * y_b

    ubuf_s[HALO:HALO + MIX_TS, :] = proj(COL_CFA, D_MODEL) * jax.nn.sigmoid(proj(COL_CFB, D_MODEL))
    _causal_dwconv(ubuf_s, cfw_ref, CF_KERNEL, conv_s)
    ubuf_s[0:HALO, :] = ubuf_s[MIX_TS:MIX_TS + HALO, :]
    c = conv_s[...] + cfb_ref[...]
    mu = jnp.mean(c, axis=-1, keepdims=True)
    dev = c - mu
    var = jnp.mean(dev * dev, axis=-1, keepdims=True)
    ln = (dev * lax.rsqrt(var + LN_EPS)) * lng_ref[...] + lnb_ref[...]
    y_c = jnp.dot(_silu(ln).astype(bf16), wco_ref[...], preferred_element_type=f32)
    m_s[...] += jax.nn.sigmoid(proj(COL_GATE + 2 * D_MODEL, D_MODEL)) * y_c

    mo = jnp.dot(m_s[...].astype(bf16), wo_ref[...], preferred_element_type=f32)
    o_ref[...] = x_ref[...] + _rms(mo, gpost_ref[...])


def _decay_tables():
    log_g = jnp.log(1.0 - 2.0 ** (-5.0 - jnp.arange(RET_HEADS, dtype=f32)))
    idx = jnp.arange(MIX_TS, dtype=f32)
    dist = jnp.abs(idx[:, None] - idx[None, :])
    chunk_id = jnp.arange(MIX_TS) // CHUNK
    visible = (chunk_id[None, :] <= chunk_id[:, None]).astype(f32)
    dmask = jnp.exp(log_g[:, None, None] * dist[None]) * visible[None]
    xi = jnp.exp(log_g[:, None] * (idx[None, :] + 1.0))
    zeta = jnp.exp(log_g[:, None] * (MIX_TS - 1.0 - idx[None, :]))
    g_l = jnp.exp(log_g * MIX_TS)
    xi = jnp.broadcast_to(xi[:, :, None], (RET_HEADS, MIX_TS, RET_V_DIM))
    zeta = jnp.broadcast_to(zeta[:, :, None], (RET_HEADS, MIX_TS, RET_QK_DIM))
    g_l = jnp.broadcast_to(g_l[:, None, None], (RET_HEADS, 1, RET_V_DIM))
    return dmask, xi, zeta, g_l


def _mixer(x, cos_t, sin_t, g_pre, g_post, w_in, w_ret_o, w_sc_o, w_cf_o, w_o,
           sc_conv_w, cf_dw_w, cf_dw_b, cf_ln_g, cf_ln_b, tables):
    b, s, d = x.shape
    dmask, xi, zeta, g_l = tables

    def resident(shape):
        zeros = (0,) * len(shape)
        return pl.BlockSpec(shape, lambda i, j: zeros, pipeline_mode=pl.Buffered(1))

    tile = lambda width: pl.BlockSpec((None, MIX_TS, width), lambda i, j: (i, j, 0))
    row = lambda v: v[None, :]
    return pl.pallas_call(
        _mixer_body,
        grid=(b, s // MIX_TS),
        in_specs=[tile(d), tile(RET_QK_DIM), tile(RET_QK_DIM),
                  resident((1, d)), resident((1, d)),
                  resident((d, D_IN)), resident((RET_V, d)), resident((d, d)), resident((d, d)), resident((d, d)),
                  resident((SC_KERNEL, d)), resident((CF_KERNEL, d)),
                  resident((1, d)), resident((1, d)), resident((1, d)),
                  resident(dmask.shape), resident(xi.shape), resident(zeta.shape), resident(g_l.shape)],
        out_specs=tile(d),
        out_shape=jax.ShapeDtypeStruct((b, s, d), f32),
        scratch_shapes=[pltpu.VMEM((MIX_TS, d), bf16),
                        pltpu.VMEM((MIX_TS, RET_V), bf16),
                        pltpu.VMEM((RET_HEADS, RET_QK_DIM, RET_V_DIM), f32),
                        pltpu.VMEM((HALO + MIX_TS, d), f32),
                        pltpu.VMEM((HALO + MIX_TS, d), f32),
                        pltpu.VMEM((MIX_TS, d), f32),
                        pltpu.VMEM((MIX_TS, d), f32)],
        compiler_params=pltpu.CompilerParams(
            dimension_semantics=("arbitrary", "arbitrary"),
            vmem_limit_bytes=V7X_VMEM_LIMIT_BYTES),
        name="mixer",
    )(x, cos_t, sin_t, row(g_pre), row(g_post), w_in, w_ret_o, w_sc_o, w_cf_o, w_o,
      sc_conv_w, cf_dw_w, row(cf_dw_b), row(cf_ln_g), row(cf_ln_b), dmask, xi, zeta, g_l)


def kernel(x, positions, norm_g, ffn1_w_gu, ffn1_w_down, w_in, w_ret_o, sc_conv_w, w_sc_o, cf_dw_w, cf_dw_b,
           cf_ln_g, cf_ln_b, w_cf_o, w_o, ffn2_w_gu, ffn2_w_down):
    b, s, d = x.shape
    assert d == D_MODEL and s % MIX_TS == 0 and (b * s) % FFN_TM == 0 and MIX_TS % CHUNK == 0
    cos_t, sin_t = _rope_tables(positions)
    tables = _decay_tables()
    cast = lambda w: w.astype(bf16)
    for l in range(DEPTH):
        g = norm_g[l]
        x = _ffn(x.reshape(b * s, d), g[0], g[1], cast(ffn1_w_gu[l]), cast(ffn1_w_down[l])).reshape(b, s, d)
        x = _mixer(x, cos_t, sin_t, g[2], g[3], cast(w_in[l]), cast(w_ret_o[l]), cast(w_sc_o[l]), cast(w_cf_o[l]),
                   cast(w_o[l]), sc_conv_w[l], cf_dw_w[l], cf_dw_b[l], cf_ln_g[l], cf_ln_b[l], tables)
        x = _ffn(x.reshape(b * s, d), g[4], g[5], cast(ffn2_w_gu[l]), cast(ffn2_w_down[l])).reshape(b, s, d)
    return x
```

```python
import jax
import jax.numpy as jnp
from jax import lax
from jax.experimental import pallas as pl
from jax.experimental.pallas import tpu as pltpu

f32 = jnp.float32
bf16 = jnp.bfloat16

D_MODEL = 1024
DEPTH = 2
CHUNK = 64
RET_HEADS = 4
RET_QK_DIM = 128
RET_V_DIM = 256
RET_QK = RET_HEADS * RET_QK_DIM
RET_V = RET_HEADS * RET_V_DIM
SC_KERNEL = 3
CF_KERNEL = 31
D_FF = 4 * D_MODEL
ROPE_BASE = 10000.0
NORM_EPS = 1e-6
LN_EPS = 1e-5

V7X_VMEM_LIMIT_BYTES = 56 * 1024 * 1024
SUBLANES = 8
SLAB = 512
FFN_TM = 512
MIX_TS = 256
HALO = 32
CONV_ROWS = 64
CONV_LANES = 128
CONV_SPLIT = 8
ROPE_TS = 512

D_PANELS = D_MODEL // SLAB
P_Q = 0
P_K = P_Q + RET_QK // SLAB
P_V = P_K + RET_QK // SLAB
P_G = P_V + RET_V // SLAB
P_SCB = P_G + RET_V // SLAB
P_SCC = P_SCB + D_PANELS
P_SCX = P_SCC + D_PANELS
P_CFA = P_SCX + D_PANELS
P_CFB = P_CFA + D_PANELS
P_GATE = P_CFB + D_PANELS
IN_PANELS = P_GATE + 3 * D_PANELS
FF_PANELS = D_FF // SLAB
HEADS_PER_PANEL = SLAB // RET_V_DIM


def _silu(x):
    return x * jax.nn.sigmoid(x)


def _panel(p):
    return slice(p * SLAB, (p + 1) * SLAB)


def _resident(shape, layer=None):
    if layer is None:
        idx = (0,) * len(shape)
        return pl.BlockSpec(shape, lambda i, j: idx, pipeline_mode=pl.Buffered(1))
    idx = (layer,) + (0,) * len(shape)
    return pl.BlockSpec((None,) + tuple(shape), lambda i, j: idx, pipeline_mode=pl.Buffered(1))


def _panels(w):
    depth, k, n = w.shape
    return w.astype(bf16).reshape(depth, k, n // SLAB, SLAB).transpose(0, 2, 1, 3)


def _rope_body(pos_ref, inv_ref, sign_ref, cos_ref, sin_ref):
    ang = pos_ref[...].astype(f32) * inv_ref[...]
    cos_ref[...] = jnp.cos(ang)
    sin_ref[...] = jnp.sin(ang) * sign_ref[...]


def _rope_tables(positions):
    b, s = positions.shape
    half = RET_QK_DIM // 2
    inv = ROPE_BASE ** (-jnp.arange(half, dtype=f32) / half)
    inv2 = jnp.concatenate([inv, inv])[None, :]
    sign = jnp.concatenate([-jnp.ones((half,), f32), jnp.ones((half,), f32)])[None, :]
    out = jax.ShapeDtypeStruct((b, s, RET_QK_DIM), f32)
    return pl.pallas_call(
        _rope_body,
        grid=(b, s // ROPE_TS),
        in_specs=[pl.BlockSpec((None, ROPE_TS, 1), lambda i, j: (i, j, 0)),
                  pl.BlockSpec((1, RET_QK_DIM), lambda i, j: (0, 0)),
                  pl.BlockSpec((1, RET_QK_DIM), lambda i, j: (0, 0))],
        out_specs=[pl.BlockSpec((None, ROPE_TS, RET_QK_DIM), lambda i, j: (i, j, 0))] * 2,
        out_shape=[out, out],
        name="rope_tables",
    )(positions[:, :, None], inv2, sign)


def _rms_to(x_ref, g_ref, dst_ref):
    x = x_ref[...]
    ms = jnp.mean(x * x, axis=-1, keepdims=True)
    dst_ref[...] = ((x * lax.rsqrt(ms + NORM_EPS)) * g_ref[...]).astype(bf16)


def _project_norm_residual(lhs_ref, w_ref, x_ref, g_ref, o_ref, scale):
    ys = [jnp.dot(lhs_ref[...], w_ref[p], preferred_element_type=f32) for p in range(D_PANELS)]
    ms = sum(jnp.sum(y * y, axis=-1, keepdims=True) for y in ys) * (1.0 / D_MODEL)
    inv = lax.rsqrt(ms + NORM_EPS)
    for p, y in enumerate(ys):
        o_ref[:, _panel(p)] = x_ref[:, _panel(p)] + scale * ((y * inv) * g_ref[:, _panel(p)])


def _ffn_body(x_ref, gpre_ref, gpost_ref, wgu_ref, wd_ref, o_ref, h_s, a_s):
    _rms_to(x_ref, gpre_ref, h_s)
    for p in range(FF_PANELS):
        g = jnp.dot(h_s[...], wgu_ref[p], preferred_element_type=f32)
        u = jnp.dot(h_s[...], wgu_ref[FF_PANELS + p], preferred_element_type=f32)
        a_s[:, _panel(p)] = (_silu(g) * u).astype(bf16)
    _project_norm_residual(a_s, wd_ref, x_ref, gpost_ref, o_ref, 0.5)


def _ffn(x2, g_pre, g_post, w_gu, w_down, layer):
    t, d = x2.shape
    tile = pl.BlockSpec((FFN_TM, d), lambda i, j: (i, 0))
    return pl.pallas_call(
        _ffn_body,
        grid=(t // FFN_TM, 1),
        in_specs=[tile, _resident((1, d)), _resident((1, d)),
                  _resident((2 * FF_PANELS, d, SLAB), layer), _resident((D_PANELS, D_FF, SLAB), layer)],
        out_specs=tile,
        out_shape=jax.ShapeDtypeStruct((t, d), f32),
        scratch_shapes=[pltpu.VMEM((FFN_TM, d), bf16),
                        pltpu.VMEM((FFN_TM, D_FF), bf16)],
        compiler_params=pltpu.CompilerParams(
            dimension_semantics=("arbitrary", "arbitrary"),
            vmem_limit_bytes=V7X_VMEM_LIMIT_BYTES),
        name="ffn",
    )(x2, g_pre[None, :], g_post[None, :], w_gu, w_down)


def _causal_dwconv(buf_ref, w_ref, ksize, emit, lane0=0, lane1=D_MODEL):
    halo = SUBLANES * pl.cdiv(ksize - 1, SUBLANES)
    for c0 in range(lane0, lane1, CONV_LANES):
        lanes = slice(c0, c0 + CONV_LANES)
        for r0 in range(0, MIX_TS, CONV_ROWS):
            win = buf_ref[HALO - halo + r0:HALO + r0 + CONV_ROWS, lanes]
            acc = None
            for r in range(min(SUBLANES, ksize)):
                rolled = win if r == 0 else pltpu.roll(win, r, axis=0)
                for a in range(halo // SUBLANES):
                    s = SUBLANES * a + r
                    if s > ksize - 1:
                        continue
                    start = halo - SUBLANES * a
                    term = w_ref[ksize - 1 - s:ksize - s, lanes] * rolled[start:start + CONV_ROWS]
                    acc = term if acc is None else acc + term
            emit(slice(r0, r0 + CONV_ROWS), lanes, acc)


def _mixer_body(x_ref, cos_ref, sin_ref, gpre_ref, gpost_ref, win_ref, wro_ref, wso_ref, wco_ref, wo_ref,
                scw_ref, cfw_ref, cfb_ref, lng_ref, lnb_ref, dmask_ref, xi_ref, zeta_ref, gl_ref,
                o_ref, h_s, q_s, kt_s, kz_s, v_s, lhs_s, state_s, zbuf_s, ubuf_s, convc_s, m_s):
    t = pl.program_id(1)

    @pl.when(t == 0)
    def _():
        state_s[...] = jnp.zeros_like(state_s)
        zbuf_s[0:HALO, :] = jnp.zeros((HALO, D_MODEL), f32)
        ubuf_s[0:HALO, :] = jnp.zeros((HALO, D_MODEL), f32)

    _rms_to(x_ref, gpre_ref, h_s)

    def proj(panel):
        return jnp.dot(h_s[...], win_ref[panel], preferred_element_type=f32)

    def retention_head(hd, g_head):
        dk = slice(hd * RET_QK_DIM, (hd + 1) * RET_QK_DIM)
        dv = slice(hd * RET_V_DIM, (hd + 1) * RET_V_DIM)
        qb = q_s[:, dk]
        vh = v_s[:, dv]
        scores = jnp.dot(qb, kt_s[dk, :], preferred_element_type=f32) * dmask_ref[hd]
        state = state_s[hd]
        o = (jnp.dot(scores.astype(bf16), vh, preferred_element_type=f32)
             + jnp.dot(qb, state.astype(bf16), preferred_element_type=f32) * xi_ref[hd])
        state_s[hd] = state * gl_ref[hd] + lax.dot_general(kz_s[:, dk], vh, (((0,), (0,)), ((), ())),
                                                           preferred_element_type=f32)
        mu = jnp.mean(o, axis=-1, keepdims=True)
        dev = o - mu
        var = jnp.mean(dev * dev, axis=-1, keepdims=True)
        lhs_s[:, dv] = (_silu(g_head) * (dev * lax.rsqrt(var + LN_EPS))).astype(bf16)

    def gated_branch(panel0, w_ref, first):
        for p in range(D_PANELS):
            y = jax.nn.sigmoid(proj(panel0 + p)) * jnp.dot(lhs_s[...], w_ref[p], preferred_element_type=f32)
            m_s[:, _panel(p)] = y if first else m_s[:, _panel(p)] + y

    for p in range(D_PANELS):
        ubuf_s[HALO:HALO + MIX_TS, _panel(p)] = proj(P_CFA + p) * jax.nn.sigmoid(proj(P_CFB + p))

    def store_convc(rows, lanes, acc):
        convc_s[rows, lanes] = acc

    conv_blocks = iter(range(0, D_MODEL, D_MODEL // CONV_SPLIT))

    def conv_c_block(n=1):
        for _ in range(n):
            lane0 = next(conv_blocks)
            _causal_dwconv(ubuf_s, cfw_ref, CF_KERNEL, store_convc, lane0, lane0 + D_MODEL // CONV_SPLIT)

    conv_c_block(2)

    cos = cos_ref[...]
    sin = sin_ref[...]
    q = proj(P_Q)
    k = proj(P_K)
    for hd in range(RET_HEADS):
        dk = slice(hd * RET_QK_DIM, (hd + 1) * RET_QK_DIM)
        qh = q[:, dk]
        kh = k[:, dk]
        kr = (kh * cos + pltpu.roll(kh, RET_QK_DIM // 2, axis=1) * sin) * (RET_QK_DIM ** -0.5)
        q_s[:, dk] = (qh * cos + pltpu.roll(qh, RET_QK_DIM // 2, axis=1) * sin).astype(bf16)
        kt_s[dk, :] = kr.T.astype(bf16)
        kz_s[:, dk] = (kr * zeta_ref[hd]).astype(bf16)
    conv_c_block()
    for p in range(RET_V // SLAB):
        v_s[:, _panel(p)] = proj(P_V + p).astype(bf16)
    conv_c_block()

    for p in range(D_PANELS):
        zbuf_s[HALO:HALO + MIX_TS, _panel(p)] = proj(P_SCC + p) * proj(P_SCX + p)
    sc_b = jnp.concatenate([proj(P_SCB + p) for p in range(D_PANELS)], axis=1)

    def store_gated_convb(rows, lanes, acc):
        lhs_s[rows, lanes] = (sc_b[rows, lanes] * acc).astype(bf16)

    _causal_dwconv(zbuf_s, scw_ref, SC_KERNEL, store_gated_convb)
    zbuf_s[0:HALO, :] = zbuf_s[MIX_TS:MIX_TS + HALO, :]
    gated_branch(P_GATE + D_PANELS, wso_ref, True)

    for gp in range(RET_V // SLAB):
        g = proj(P_G + gp)
        for i in range(HEADS_PER_PANEL):
            retention_head(gp * HEADS_PER_PANEL + i, g[:, i * RET_V_DIM:(i + 1) * RET_V_DIM])
            conv_c_block()
    ubuf_s[0:HALO, :] = ubuf_s[MIX_TS:MIX_TS + HALO, :]
    gated_branch(P_GATE, wro_ref, False)

    c = convc_s[...] + cfb_ref[...]
    mu = jnp.mean(c, axis=-1, keepdims=True)
    dev = c - mu
    var = jnp.mean(dev * dev, axis=-1, keepdims=True)
    lhs_s[...] = _silu((dev * lax.rsqrt(var + LN_EPS)) * lng_ref[...] + lnb_ref[...]).astype(bf16)
    gated_branch(P_GATE + 2 * D_PANELS, wco_ref, False)

    lhs_s[...] = m_s[...].astype(bf16)
    _project_norm_residual(lhs_s, wo_ref, x_ref, gpost_ref, o_ref, 1.0)


def _decay_tables():
    log_g = jnp.log(1.0 - 2.0 ** (-5.0 - jnp.arange(RET_HEADS, dtype=f32)))
    idx = jnp.arange(MIX_TS, dtype=f32)
    dist = jnp.abs(idx[:, None] - idx[None, :])
    chunk_id = jnp.arange(MIX_TS) // CHUNK
    visible = (chunk_id[None, :] <= chunk_id[:, None]).astype(f32)
    dmask = jnp.exp(log_g[:, None, None] * dist[None]) * visible[None]
    xi = jnp.exp(log_g[:, None] * (idx[None, :] + 1.0))
    zeta = jnp.exp(log_g[:, None] * (MIX_TS - 1.0 - idx[None, :]))
    g_l = jnp.exp(log_g * MIX_TS)
    xi = jnp.broadcast_to(xi[:, :, None], (RET_HEADS, MIX_TS, RET_V_DIM))
    zeta = jnp.broadcast_to(zeta[:, :, None], (RET_HEADS, MIX_TS, RET_QK_DIM))
    g_l = jnp.broadcast_to(g_l[:, None, None], (RET_HEADS, 1, RET_V_DIM))
    return dmask, xi, zeta, g_l


def _mixer(x, cos_t, sin_t, g_pre, g_post, w_in, w_ret_o, w_sc_o, w_cf_o, w_o,
           sc_conv_w, cf_dw_w, cf_dw_b, cf_ln_g, cf_ln_b, tables, layer):
    b, s, d = x.shape
    dmask, xi, zeta, g_l = tables
    tile = lambda width: pl.BlockSpec((None, MIX_TS, width), lambda i, j: (i, j, 0))
    row = lambda v: v[None, :]
    rows = HALO + MIX_TS
    return pl.pallas_call(
        _mixer_body,
        grid=(b, s // MIX_TS),
        in_specs=[tile(d), tile(RET_QK_DIM), tile(RET_QK_DIM),
                  _resident((1, d)), _resident((1, d)),
                  _resident((IN_PANELS, d, SLAB), layer), _resident((D_PANELS, RET_V, SLAB), layer),
                  _resident((D_PANELS, d, SLAB), layer), _resident((D_PANELS, d, SLAB), layer),
                  _resident((D_PANELS, d, SLAB), layer),
                  _resident((SC_KERNEL, d)), _resident((CF_KERNEL, d)),
                  _resident((1, d)), _resident((1, d)), _resident((1, d)),
                  _resident(dmask.shape), _resident(xi.shape), _resident(zeta.shape), _resident(g_l.shape)],
        out_specs=tile(d),
        out_shape=jax.ShapeDtypeStruct((b, s, d), f32),
        scratch_shapes=[pltpu.VMEM((MIX_TS, d), bf16),
                        pltpu.VMEM((MIX_TS, RET_QK), bf16),
                        pltpu.VMEM((RET_QK, MIX_TS), bf16),
                        pltpu.VMEM((MIX_TS, RET_QK), bf16),
                        pltpu.VMEM((MIX_TS, RET_V), bf16),
                        pltpu.VMEM((MIX_TS, d), bf16),
                        pltpu.VMEM((RET_HEADS, RET_QK_DIM, RET_V_DIM), f32),
                        pltpu.VMEM((rows, d), f32),
                        pltpu.VMEM((rows, d), f32),
                        pltpu.VMEM((MIX_TS, d), f32),
                        pltpu.VMEM((MIX_TS, d), f32)],
        compiler_params=pltpu.CompilerParams(
            dimension_semantics=("arbitrary", "arbitrary"),
            vmem_limit_bytes=V7X_VMEM_LIMIT_BYTES),
        name="mixer",
    )(x, cos_t, sin_t, row(g_pre), row(g_post), w_in, w_ret_o, w_sc_o, w_cf_o, w_o,
      sc_conv_w, cf_dw_w, row(cf_dw_b), row(cf_ln_g), row(cf_ln_b), dmask, xi, zeta, g_l)


def kernel(x, positions, norm_g, ffn1_w_gu, ffn1_w_down, w_in, w_ret_o, sc_conv_w, w_sc_o, cf_dw_w, cf_dw_b,
           cf_ln_g, cf_ln_b, w_cf_o, w_o, ffn2_w_gu, ffn2_w_down):
    b, s, d = x.shape
    assert d == D_MODEL and s % MIX_TS == 0 and (b * s) % FFN_TM == 0 and MIX_TS % CHUNK == 0
    cos_t, sin_t = _rope_tables(positions)
    tables = _decay_tables()
    ffn1_w_gu, ffn1_w_down, ffn2_w_gu, ffn2_w_down = map(_panels, (ffn1_w_gu, ffn1_w_down, ffn2_w_gu, ffn2_w_down))
    w_in, w_ret_o, w_sc_o, w_cf_o, w_o = map(_panels, (w_in, w_ret_o, w_sc_o, w_cf_o, w_o))
    for l in range(DEPTH):
        g = norm_g[l]
        x = _ffn(x.reshape(b * s, d), g[0], g[1], ffn1_w_gu, ffn1_w_down, l).reshape(b, s, d)
        x = _mixer(x, cos_t, sin_t, g[2], g[3], w_in, w_ret_o, w_sc_o, w_cf_o, w_o,
                   sc_conv_w[l], cf_dw_w[l], cf_dw_b[l], cf_ln_g[l], cf_ln_b[l], tables, l)
        x = _ffn(x.reshape(b * s, d), g[4], g[5], ffn2_w_gu, ffn2_w_down, l).reshape(b, s, d)
    return x
```

```python
import jax
import jax.numpy as jnp
from jax import lax
from jax.experimental import pallas as pl
from jax.experimental.pallas import tpu as pltpu

f32 = jnp.float32
bf16 = jnp.bfloat16

D_MODEL = 1024
DEPTH = 2
CHUNK = 64
RET_HEADS = 4
RET_QK_DIM = 128
RET_V_DIM = 256
RET_QK = RET_HEADS * RET_QK_DIM
RET_V = RET_HEADS * RET_V_DIM
SC_KERNEL = 3
CF_KERNEL = 31
D_FF = 4 * D_MODEL
ROPE_BASE = 10000.0
NORM_EPS = 1e-6
LN_EPS = 1e-5

V7X_VMEM_LIMIT_BYTES = 60 * 1024 * 1024
SUBLANES = 8
SLAB = 512
FFN_TM = 512
FFN_PARTS = 2
MIX_TS = 512
RET_L = 256
HALO = 32
CONV_ROWS = 64
CONV_LANES = 128
CONV_SPLIT = 8
ROPE_TS = 512

D_PANELS = D_MODEL // SLAB
P_Q = 0
P_K = P_Q + RET_QK // SLAB
P_V = P_K + RET_QK // SLAB
P_G = P_V + RET_V // SLAB
P_SCB = P_G + RET_V // SLAB
P_SCC = P_SCB + D_PANELS
P_SCX = P_SCC + D_PANELS
P_CFA = P_SCX + D_PANELS
P_CFB = P_CFA + D_PANELS
P_GATE = P_CFB + D_PANELS
IN_PANELS = P_GATE + 3 * D_PANELS
FF_PANELS = D_FF // SLAB
HEADS_PER_PANEL = SLAB // RET_V_DIM


def _silu(x):
    return x * jax.nn.sigmoid(x)


def _panel(p):
    return slice(p * SLAB, (p + 1) * SLAB)


def _resident(shape, layer=None):
    if layer is None:
        idx = (0,) * len(shape)
        return pl.BlockSpec(shape, lambda i, j: idx, pipeline_mode=pl.Buffered(1))
    idx = (layer,) + (0,) * len(shape)
    return pl.BlockSpec((None,) + tuple(shape), lambda i, j: idx, pipeline_mode=pl.Buffered(1))


def _panels(w):
    depth, k, n = w.shape
    return w.astype(bf16).reshape(depth, k, n // SLAB, SLAB).transpose(0, 2, 1, 3)


def _rope_body(pos_ref, inv_ref, sign_ref, cos_ref, sin_ref):
    ang = pos_ref[...].astype(f32) * inv_ref[...]
    cos_ref[...] = jnp.cos(ang)
    sin_ref[...] = jnp.sin(ang) * sign_ref[...]


def _rope_tables(positions):
    b, s = positions.shape
    half = RET_QK_DIM // 2
    inv = ROPE_BASE ** (-jnp.arange(half, dtype=f32) / half)
    inv2 = jnp.concatenate([inv, inv])[None, :]
    sign = jnp.concatenate([-jnp.ones((half,), f32), jnp.ones((half,), f32)])[None, :]
    out = jax.ShapeDtypeStruct((b, s, RET_QK_DIM), f32)
    return pl.pallas_call(
        _rope_body,
        grid=(b, s // ROPE_TS),
        in_specs=[pl.BlockSpec((None, ROPE_TS, 1), lambda i, j: (i, j, 0)),
                  pl.BlockSpec((1, RET_QK_DIM), lambda i, j: (0, 0)),
                  pl.BlockSpec((1, RET_QK_DIM), lambda i, j: (0, 0))],
        out_specs=[pl.BlockSpec((None, ROPE_TS, RET_QK_DIM), lambda i, j: (i, j, 0))] * 2,
        out_shape=[out, out],
        name="rope_tables",
    )(positions[:, :, None], inv2, sign)


def _rms_scale(x, g):
    ms = jnp.mean(x * x, axis=-1, keepdims=True)
    return (x * lax.rsqrt(ms + NORM_EPS)) * g


def _norm_residual(ys, x_ref, g_ref, o_ref, rows, scale):
    ms = sum(jnp.sum(y * y, axis=-1, keepdims=True) for y in ys) * (1.0 / D_MODEL)
    inv = lax.rsqrt(ms + NORM_EPS)
    for p, y in enumerate(ys):
        o_ref[rows, _panel(p)] = x_ref[rows, _panel(p)] + scale * ((y * inv) * g_ref[:, _panel(p)])


def _ffn_body(x_ref, gpre_ref, gpost_ref, wgu_ref, wd_ref, o_ref, h_s, a_s):
    part = FFN_TM // FFN_PARTS
    parts = [slice(i * part, (i + 1) * part) for i in range(FFN_PARTS)]
    for rows in parts:
        h_s[rows, :] = _rms_scale(x_ref[rows, :], gpre_ref[...]).astype(bf16)
    for p in range(FF_PANELS):
        for rows in parts:
            g = jnp.dot(h_s[rows, :], wgu_ref[:, _panel(p)], preferred_element_type=f32)
            u = jnp.dot(h_s[rows, :], wgu_ref[:, _panel(FF_PANELS + p)], preferred_element_type=f32)
            a_s[rows, _panel(p)] = (_silu(g) * u).astype(bf16)
    for rows in parts:
        ys = [jnp.dot(a_s[rows, :], wd_ref[:, _panel(p)], preferred_element_type=f32) for p in range(D_PANELS)]
        _norm_residual(ys, x_ref, gpost_ref, o_ref, rows, 0.5)


def _ffn(x2, g_pre, g_post, w_gu, w_down, layer):
    t, d = x2.shape
    tile = pl.BlockSpec((FFN_TM, d), lambda i, j: (i, 0))
    return pl.pallas_call(
        _ffn_body,
        grid=(t // FFN_TM, 1),
        in_specs=[tile, _resident((1, d)), _resident((1, d)),
                  _resident((d, 2 * D_FF), layer), _resident((D_FF, d), layer)],
        out_specs=tile,
        out_shape=jax.ShapeDtypeStruct((t, d), f32),
        scratch_shapes=[pltpu.VMEM((FFN_TM, d), bf16),
                        pltpu.VMEM((FFN_TM, D_FF), bf16)],
        compiler_params=pltpu.CompilerParams(
            dimension_semantics=("arbitrary", "arbitrary"),
            vmem_limit_bytes=V7X_VMEM_LIMIT_BYTES),
        name="ffn",
    )(x2, g_pre[None, :], g_post[None, :], w_gu, w_down)


def _causal_dwconv(buf_ref, w_ref, ksize, emit, lane0=0, lane1=D_MODEL):
    halo = SUBLANES * pl.cdiv(ksize - 1, SUBLANES)
    for c0 in range(lane0, lane1, CONV_LANES):
        lanes = slice(c0, c0 + CONV_LANES)
        for r0 in range(0, MIX_TS, CONV_ROWS):
            win = buf_ref[HALO - halo + r0:HALO + r0 + CONV_ROWS, lanes]
            acc = None
            for r in range(min(SUBLANES, ksize)):
                rolled = win if r == 0 else pltpu.roll(win, r, axis=0)
                for a in range(halo // SUBLANES):
                    s = SUBLANES * a + r
                    if s > ksize - 1:
                        continue
                    start = halo - SUBLANES * a
                    term = w_ref[ksize - 1 - s:ksize - s, lanes] * rolled[start:start + CONV_ROWS]
                    acc = term if acc is None else acc + term
            emit(slice(r0, r0 + CONV_ROWS), lanes, acc)


def _mixer_body(x_ref, cos_ref, sin_ref, gpre_ref, gpost_ref, win_ref, wro_ref, wso_ref, wco_ref, wo_ref,
                scw_ref, cfw_ref, cfb_ref, lng_ref, lnb_ref, dmask_ref, xi_ref, zeta_ref, gl_ref,
                o_ref, h_s, q_s, kt_s, kz_s, v_s, lhs_s, state_s, zbuf_s, ubuf_s, convc_s, m_s):
    t = pl.program_id(1)

    @pl.when(t == 0)
    def _():
        state_s[...] = jnp.zeros_like(state_s)
        zbuf_s[0:HALO, :] = jnp.zeros((HALO, D_MODEL), f32)
        ubuf_s[0:HALO, :] = jnp.zeros((HALO, D_MODEL), f32)

    h_s[...] = _rms_scale(x_ref[...], gpre_ref[...]).astype(bf16)

    def proj(panel):
        return jnp.dot(h_s[...], win_ref[panel], preferred_element_type=f32)

    def retention_head(hd, g_head):
        dk = slice(hd * RET_QK_DIM, (hd + 1) * RET_QK_DIM)
        dv = slice(hd * RET_V_DIM, (hd + 1) * RET_V_DIM)
        xi = jnp.concatenate([xi_ref[hd]] * (RET_V_DIM // RET_QK_DIM), axis=1)
        for c in range(MIX_TS // RET_L):
            rows = slice(c * RET_L, (c + 1) * RET_L)
            qb = q_s[rows, dk]
            vh = v_s[rows, dv]
            scores = jnp.dot(qb, kt_s[c, dk, :], preferred_element_type=f32) * dmask_ref[hd]
            state = state_s[hd]
            o = (jnp.dot(scores.astype(bf16), vh, preferred_element_type=f32)
                 + jnp.dot(qb, state.astype(bf16), preferred_element_type=f32) * xi)
            state_s[hd] = state * gl_ref[hd] + lax.dot_general(kz_s[rows, dk], vh, (((0,), (0,)), ((), ())),
                                                               preferred_element_type=f32)
            mu = jnp.mean(o, axis=-1, keepdims=True)
            dev = o - mu
            var = jnp.mean(dev * dev, axis=-1, keepdims=True)
            lhs_s[rows, dv] = (_silu(g_head[rows]) * (dev * lax.rsqrt(var + LN_EPS))).astype(bf16)

    def gated_branch(panel0, w_ref, first):
        for p in range(D_PANELS):
            y = jax.nn.sigmoid(proj(panel0 + p)) * jnp.dot(lhs_s[...], w_ref[p], preferred_element_type=f32)
            m_s[:, _panel(p)] = y if first else m_s[:, _panel(p)] + y

    for p in range(D_PANELS):
        ubuf_s[HALO:HALO + MIX_TS, _panel(p)] = proj(P_CFA + p) * jax.nn.sigmoid(proj(P_CFB + p))

    def store_convc(rows, lanes, acc):
        convc_s[rows, lanes] = acc

    conv_blocks = iter(range(0, D_MODEL, D_MODEL // CONV_SPLIT))

    def conv_c_block(n=1):
        for _ in range(n):
            lane0 = next(conv_blocks)
            _causal_dwconv(ubuf_s, cfw_ref, CF_KERNEL, store_convc, lane0, lane0 + D_MODEL // CONV_SPLIT)

    conv_c_block(2)

    cos = cos_ref[...]
    sin = sin_ref[...]
    q = proj(P_Q)
    k = proj(P_K)
    for hd in range(RET_HEADS):
        dk = slice(hd * RET_QK_DIM, (hd + 1) * RET_QK_DIM)
        qh = q[:, dk]
        kh = k[:, dk]
        kr = (kh * cos + pltpu.roll(kh, RET_QK_DIM // 2, axis=1) * sin) * (RET_QK_DIM ** -0.5)
        q_s[:, dk] = (qh * cos + pltpu.roll(qh, RET_QK_DIM // 2, axis=1) * sin).astype(bf16)
        for c in range(MIX_TS // RET_L):
            rows = slice(c * RET_L, (c + 1) * RET_L)
            kt_s[c, dk, :] = kr[rows].T.astype(bf16)
            kz_s[rows, dk] = (kr[rows] * zeta_ref[hd]).astype(bf16)
    conv_c_block()
    for p in range(RET_V // SLAB):
        v_s[:, _panel(p)] = proj(P_V + p).astype(bf16)
    conv_c_block()

    for p in range(D_PANELS):
        zbuf_s[HALO:HALO + MIX_TS, _panel(p)] = proj(P_SCC + p) * proj(P_SCX + p)
    sc_b = jnp.concatenate([proj(P_SCB + p) for p in range(D_PANELS)], axis=1)

    def store_gated_convb(rows, lanes, acc):
        lhs_s[rows, lanes] = (sc_b[rows, lanes] * acc).astype(bf16)

    _causal_dwconv(zbuf_s, scw_ref, SC_KERNEL, store_gated_convb)
    zbuf_s[0:HALO, :] = zbuf_s[MIX_TS:MIX_TS + HALO, :]
    gated_branch(P_GATE + D_PANELS, wso_ref, True)

    for gp in range(RET_V // SLAB):
        g = proj(P_G + gp)
        for i in range(HEADS_PER_PANEL):
            retention_head(gp * HEADS_PER_PANEL + i, g[:, i * RET_V_DIM:(i + 1) * RET_V_DIM])
            conv_c_block()
    ubuf_s[0:HALO, :] = ubuf_s[MIX_TS:MIX_TS + HALO, :]
    gated_branch(P_GATE, wro_ref, False)

    c = convc_s[...] + cfb_ref[...]
    mu = jnp.mean(c, axis=-1, keepdims=True)
    dev = c - mu
    var = jnp.mean(dev * dev, axis=-1, keepdims=True)
    lhs_s[...] = _silu((dev * lax.rsqrt(var + LN_EPS)) * lng_ref[...] + lnb_ref[...]).astype(bf16)
    gated_branch(P_GATE + 2 * D_PANELS, wco_ref, False)

    lhs_s[...] = m_s[...].astype(bf16)
    ys = [jnp.dot(lhs_s[...], wo_ref[p], preferred_element_type=f32) for p in range(D_PANELS)]
    _norm_residual(ys, x_ref, gpost_ref, o_ref, slice(0, MIX_TS), 1.0)


def _decay_tables():
    log_g = jnp.log(1.0 - 2.0 ** (-5.0 - jnp.arange(RET_HEADS, dtype=f32)))
    idx = jnp.arange(RET_L, dtype=f32)
    dist = jnp.abs(idx[:, None] - idx[None, :])
    chunk_id = jnp.arange(RET_L) // CHUNK
    visible = (chunk_id[None, :] <= chunk_id[:, None]).astype(f32)
    dmask = jnp.exp(log_g[:, None, None] * dist[None]) * visible[None]
    xi = jnp.exp(log_g[:, None] * (idx[None, :] + 1.0))
    zeta = jnp.exp(log_g[:, None] * (RET_L - 1.0 - idx[None, :]))
    g_l = jnp.exp(log_g * RET_L)
    xi = jnp.broadcast_to(xi[:, :, None], (RET_HEADS, RET_L, RET_QK_DIM))
    zeta = jnp.broadcast_to(zeta[:, :, None], (RET_HEADS, RET_L, RET_QK_DIM))
    g_l = jnp.broadcast_to(g_l[:, None, None], (RET_HEADS, 1, RET_V_DIM))
    return dmask, xi, zeta, g_l


def _mixer(x, cos_t, sin_t, g_pre, g_post, w_in, w_ret_o, w_sc_o, w_cf_o, w_o,
           sc_conv_w, cf_dw_w, cf_dw_b, cf_ln_g, cf_ln_b, tables, layer):
    b, s, d = x.shape
    dmask, xi, zeta, g_l = tables
    tile = lambda width: pl.BlockSpec((None, MIX_TS, width), lambda i, j: (i, j, 0))
    row = lambda v: v[None, :]
    rows = HALO + MIX_TS
    return pl.pallas_call(
        _mixer_body,
        grid=(b, s // MIX_TS),
        in_specs=[tile(d), tile(RET_QK_DIM), tile(RET_QK_DIM),
                  _resident((1, d)), _resident((1, d)),
                  _resident((IN_PANELS, d, SLAB), layer), _resident((D_PANELS, RET_V, SLAB), layer),
                  _resident((D_PANELS, d, SLAB), layer), _resident((D_PANELS, d, SLAB), layer),
                  _resident((D_PANELS, d, SLAB), layer),
                  _resident((SC_KERNEL, d)), _resident((CF_KERNEL, d)),
                  _resident((1, d)), _resident((1, d)), _resident((1, d)),
                  _resident(dmask.shape), _resident(xi.shape), _resident(zeta.shape), _resident(g_l.shape)],
        out_specs=tile(d),
        out_shape=jax.ShapeDtypeStruct((b, s, d), f32),
        scratch_shapes=[pltpu.VMEM((MIX_TS, d), bf16),
                        pltpu.VMEM((MIX_TS, RET_QK), bf16),
                        pltpu.VMEM((MIX_TS // RET_L, RET_QK, RET_L), bf16),
                        pltpu.VMEM((MIX_TS, RET_QK), bf16),
                        pltpu.VMEM((MIX_TS, RET_V), bf16),
                        pltpu.VMEM((MIX_TS, d), bf16),
                        pltpu.VMEM((RET_HEADS, RET_QK_DIM, RET_V_DIM), f32),
                        pltpu.VMEM((rows, d), f32),
                        pltpu.VMEM((rows, d), f32),
                        pltpu.VMEM((MIX_TS, d), f32),
                        pltpu.VMEM((MIX_TS, d), f32)],
        compiler_params=pltpu.CompilerParams(
            dimension_semantics=("arbitrary", "arbitrary"),
            vmem_limit_bytes=V7X_VMEM_LIMIT_BYTES),
        name="mixer",
    )(x, cos_t, sin_t, row(g_pre), row(g_post), w_in, w_ret_o, w_sc_o, w_cf_o, w_o,
      sc_conv_w, cf_dw_w, row(cf_dw_b), row(cf_ln_g), row(cf_ln_b), dmask, xi, zeta, g_l)


def kernel(x, positions, norm_g, ffn1_w_gu, ffn1_w_down, w_in, w_ret_o, sc_conv_w, w_sc_o, cf_dw_w, cf_dw_b,
           cf_ln_g, cf_ln_b, w_cf_o, w_o, ffn2_w_gu, ffn2_w_down):
    b, s, d = x.shape
    assert d == D_MODEL and s % MIX_TS == 0 and (b * s) % FFN_TM == 0 and MIX_TS % RET_L == 0 and RET_L % CHUNK == 0
    cos_t, sin_t = _rope_tables(positions)
    tables = _decay_tables()
    cast = lambda w: w.astype(bf16)
    ffn1_w_gu, ffn1_w_down, ffn2_w_gu, ffn2_w_down = map(cast, (ffn1_w_gu, ffn1_w_down, ffn2_w_gu, ffn2_w_down))
    w_in, w_ret_o, w_sc_o, w_cf_o, w_o = map(_panels, (w_in, w_ret_o, w_sc_o, w_cf_o, w_o))
    for l in range(DEPTH):
        g = norm_g[l]
        x = _ffn(x.reshape(b * s, d), g[0], g[1], ffn1_w_gu, ffn1_w_down, l).reshape(b, s, d)
        x = _mixer(x, cos_t, sin_t, g[2], g[3], w_in, w_ret_o, w_sc_o, w_cf_o, w_o,
                   sc_conv_w[l], cf_dw_w[l], cf_dw_b[l], cf_ln_g[l], cf_ln_b[l], tables, l)
        x = _ffn(x.reshape(b * s, d), g[4], g[5], ffn2_w_gu, ffn2_w_down, l).reshape(b, s, d)
    return x
```

```python
import functools
from typing import NamedTuple

import jax
import jax.numpy as jnp
from jax import lax
from jax.experimental import pallas as pl
from jax.experimental.pallas import tpu as pltpu

f32 = jnp.float32
bf16 = jnp.bfloat16

D_MODEL = 1024
DEPTH = 2
CHUNK = 64
RET_HEADS = 4
RET_QK_DIM = 128
RET_V_DIM = 256
RET_QK = RET_HEADS * RET_QK_DIM
RET_V = RET_HEADS * RET_V_DIM
SC_KERNEL = 3
CF_KERNEL = 31
D_FF = 4 * D_MODEL
ROPE_BASE = 10000.0
NORM_EPS = 1e-6
LN_EPS = 1e-5

V7X_VMEM_LIMIT_BYTES = 60 * 1024 * 1024
SUBLANES = 8
BF16_ROWS = 16
SLAB = 512
FFN_TM = 1024
FFN_PARTS = 4
MIX_TS = 512
RET_L = 256
HALO = 32
CONV_ROWS = 64
CONV_LANES = 128
CONV_SPLIT = 8
ROPE_TS = 512

D_PANELS = D_MODEL // SLAB
P_Q = 0
P_K = P_Q + RET_QK // SLAB
P_V = P_K + RET_QK // SLAB
P_G = P_V + RET_V // SLAB
P_SCB = P_G + RET_V // SLAB
P_SCC = P_SCB + D_PANELS
P_SCX = P_SCC + D_PANELS
P_CFA = P_SCX + D_PANELS
P_CFB = P_CFA + D_PANELS
P_GATE = P_CFB + D_PANELS
IN_PANELS = P_GATE + 3 * D_PANELS
FF_PANELS = D_FF // SLAB
HEADS_PER_PANEL = SLAB // RET_V_DIM


def _silu(x):
    return x * jax.nn.sigmoid(x)


def _panel(p):
    return slice(p * SLAB, (p + 1) * SLAB)


def _resident(shape):
    idx = (0,) * len(shape)
    return pl.BlockSpec(shape, lambda i, j: idx, pipeline_mode=pl.Buffered(1))


class _Cast(NamedTuple):
    w: jax.Array
    layer: int
    panels: bool


def _cast_specs(jobs, steps, step_of):
    in_specs, out_specs, out_shapes = [], [], []
    for job in jobs:
        _, k, n = job.w.shape
        rows = k // steps
        assert rows * steps == k and rows % BF16_ROWS == 0 and n % SLAB == 0
        in_specs.append(pl.BlockSpec((None, rows, n), lambda i, j, layer=job.layer: (layer, step_of(i, j), 0)))
        if job.panels:
            out_specs.append(pl.BlockSpec((n // SLAB, rows, SLAB), lambda i, j: (0, step_of(i, j), 0)))
            out_shapes.append(jax.ShapeDtypeStruct((n // SLAB, k, SLAB), bf16))
        else:
            out_specs.append(pl.BlockSpec((rows, n), lambda i, j: (step_of(i, j), 0)))
            out_shapes.append(jax.ShapeDtypeStruct((k, n), bf16))
    return in_specs, out_specs, out_shapes


def _run_casts(src_refs, dst_refs):
    for src, dst in zip(src_refs, dst_refs):
        if len(dst.shape) == 3:
            for p in range(dst.shape[0]):
                dst[p] = src[:, _panel(p)].astype(bf16)
        else:
            dst[...] = src[...].astype(bf16)


def _rope_body(pos_ref, inv_ref, sign_ref, cos_ref, sin_ref):
    ang = pos_ref[...].astype(f32) * inv_ref[...]
    cos_ref[...] = jnp.cos(ang)
    sin_ref[...] = jnp.sin(ang) * sign_ref[...]


def _rope_tables(positions):
    b, s = positions.shape
    half = RET_QK_DIM // 2
    inv = ROPE_BASE ** (-jnp.arange(half, dtype=f32) / half)
    inv2 = jnp.concatenate([inv, inv])[None, :]
    sign = jnp.concatenate([-jnp.ones((half,), f32), jnp.ones((half,), f32)])[None, :]
    out = jax.ShapeDtypeStruct((b, s, RET_QK_DIM), f32)
    return pl.pallas_call(
        _rope_body,
        grid=(b, s // ROPE_TS),
        in_specs=[pl.BlockSpec((None, ROPE_TS, 1), lambda i, j: (i, j, 0)),
                  pl.BlockSpec((1, RET_QK_DIM), lambda i, j: (0, 0)),
                  pl.BlockSpec((1, RET_QK_DIM), lambda i, j: (0, 0))],
        out_specs=[pl.BlockSpec((None, ROPE_TS, RET_QK_DIM), lambda i, j: (i, j, 0))] * 2,
        out_shape=[out, out],
        name="rope_tables",
    )(positions[:, :, None], inv2, sign)


def _rms_scale(x, g):
    ms = jnp.mean(x * x, axis=-1, keepdims=True)
    return (x * lax.rsqrt(ms + NORM_EPS)) * g


def _norm_residual(ys, x_ref, g_ref, o_ref, rows, scale):
    ms = sum(jnp.sum(y * y, axis=-1, keepdims=True) for y in ys) * (1.0 / D_MODEL)
    inv = lax.rsqrt(ms + NORM_EPS)
    for p, y in enumerate(ys):
        o_ref[rows, _panel(p)] = x_ref[rows, _panel(p)] + scale * ((y * inv) * g_ref[:, _panel(p)])


def _ffn_body(n_cast, x_ref, gpre_ref, gpost_ref, wgu_ref, wd_ref, *refs):
    cast_src, o_ref, cast_dst, (h_s, a_s) = refs[:n_cast], refs[n_cast], refs[n_cast + 1:2 * n_cast + 1], refs[2 * n_cast + 1:]
    _run_casts(cast_src, cast_dst)
    part = FFN_TM // FFN_PARTS
    parts = [slice(i * part, (i + 1) * part) for i in range(FFN_PARTS)]
    for rows in parts:
        h_s[rows, :] = _rms_scale(x_ref[rows, :], gpre_ref[...]).astype(bf16)
    for p in range(FF_PANELS):
        for rows in parts:
            g = jnp.dot(h_s[rows, :], wgu_ref[:, _panel(p)], preferred_element_type=f32)
            u = jnp.dot(h_s[rows, :], wgu_ref[:, _panel(FF_PANELS + p)], preferred_element_type=f32)
            a_s[rows, _panel(p)] = (_silu(g) * u).astype(bf16)
    for rows in parts:
        ys = [jnp.dot(a_s[rows, :], wd_ref[:, _panel(p)], preferred_element_type=f32) for p in range(D_PANELS)]
        _norm_residual(ys, x_ref, gpost_ref, o_ref, rows, 0.5)


def _ffn(x2, g_pre, g_post, w_gu, w_down, casts=()):
    t, d = x2.shape
    steps = t // FFN_TM
    tile = pl.BlockSpec((FFN_TM, d), lambda i, j: (i, 0))
    cast_in, cast_out, cast_shapes = _cast_specs(casts, steps, lambda i, j: i)
    out = pl.pallas_call(
        functools.partial(_ffn_body, len(casts)),
        grid=(steps, 1),
        in_specs=[tile, _resident((1, d)), _resident((1, d)),
                  _resident((d, 2 * D_FF)), _resident((D_FF, d))] + cast_in,
        out_specs=[tile] + cast_out,
        out_shape=[jax.ShapeDtypeStruct((t, d), f32)] + cast_shapes,
        scratch_shapes=[pltpu.VMEM((FFN_TM, d), bf16),
                        pltpu.VMEM((FFN_TM, D_FF), bf16)],
        compiler_params=pltpu.CompilerParams(
            dimension_semantics=("arbitrary", "arbitrary"),
            vmem_limit_bytes=V7X_VMEM_LIMIT_BYTES),
        name="ffn",
    )(x2, g_pre[None, :], g_post[None, :], w_gu, w_down, *[job.w for job in casts])
    return out[0], out[1:]


def _causal_dwconv(buf_ref, w_ref, ksize, emit, lane0=0, lane1=D_MODEL):
    halo = SUBLANES * pl.cdiv(ksize - 1, SUBLANES)
    for c0 in range(lane0, lane1, CONV_LANES):
        lanes = slice(c0, c0 + CONV_LANES)
        for r0 in range(0, MIX_TS, CONV_ROWS):
            win = buf_ref[HALO - halo + r0:HALO + r0 + CONV_ROWS, lanes]
            acc = None
            for r in range(min(SUBLANES, ksize)):
                rolled = win if r == 0 else pltpu.roll(win, r, axis=0)
                for a in range(halo // SUBLANES):
                    s = SUBLANES * a + r
                    if s > ksize - 1:
                        continue
                    start = halo - SUBLANES * a
                    term = w_ref[ksize - 1 - s:ksize - s, lanes] * rolled[start:start + CONV_ROWS]
                    acc = term if acc is None else acc + term
            emit(slice(r0, r0 + CONV_ROWS), lanes, acc)


def _mixer_body(n_cast, x_ref, cos_ref, sin_ref, gpre_ref, gpost_ref, win_ref, wro_ref, wso_ref, wco_ref, wo_ref,
                scw_ref, cfw_ref, cfb_ref, lng_ref, lnb_ref, dmask_ref, xi_ref, zeta_ref, gl_ref, *refs):
    cast_src, o_ref, cast_dst = refs[:n_cast], refs[n_cast], refs[n_cast + 1:2 * n_cast + 1]
    h_s, q_s, kt_s, kz_s, v_s, lhs_s, state_s, zbuf_s, ubuf_s, convc_s, m_s = refs[2 * n_cast + 1:]
    _run_casts(cast_src, cast_dst)
    t = pl.program_id(1)

    @pl.when(t == 0)
    def _():
        state_s[...] = jnp.zeros_like(state_s)
        zbuf_s[0:HALO, :] = jnp.zeros((HALO, D_MODEL), f32)
        ubuf_s[0:HALO, :] = jnp.zeros((HALO, D_MODEL), f32)

    h_s[...] = _rms_scale(x_ref[...], gpre_ref[...]).astype(bf16)

    def proj(panel):
        return jnp.dot(h_s[...], win_ref[panel], preferred_element_type=f32)

    def retention_head(hd, g_head):
        dk = slice(hd * RET_QK_DIM, (hd + 1) * RET_QK_DIM)
        dv = slice(hd * RET_V_DIM, (hd + 1) * RET_V_DIM)
        xi = jnp.concatenate([xi_ref[hd]] * (RET_V_DIM // RET_QK_DIM), axis=1)
        for c in range(MIX_TS // RET_L):
            rows = slice(c * RET_L, (c + 1) * RET_L)
            qb = q_s[rows, dk]
            vh = v_s[rows, dv]
            scores = jnp.dot(qb, kt_s[c, dk, :], preferred_element_type=f32) * dmask_ref[hd]
            state = state_s[hd]
            o = (jnp.dot(scores.astype(bf16), vh, preferred_element_type=f32)
                 + jnp.dot(qb, state.astype(bf16), preferred_element_type=f32) * xi)
            state_s[hd] = state * gl_ref[hd] + lax.dot_general(kz_s[rows, dk], vh, (((0,), (0,)), ((), ())),
                                                               preferred_element_type=f32)
            mu = jnp.mean(o, axis=-1, keepdims=True)
            dev = o - mu
            var = jnp.mean(dev * dev, axis=-1, keepdims=True)
            lhs_s[rows, dv] = (_silu(g_head[rows]) * (dev * lax.rsqrt(var + LN_EPS))).astype(bf16)

    def gated_branch(panel0, w_ref, first):
        for p in range(D_PANELS):
            y = jax.nn.sigmoid(proj(panel0 + p)) * jnp.dot(lhs_s[...], w_ref[p], preferred_element_type=f32)
            m_s[:, _panel(p)] = y if first else m_s[:, _panel(p)] + y

    for p in range(D_PANELS):
        ubuf_s[HALO:HALO + MIX_TS, _panel(p)] = proj(P_CFA + p) * jax.nn.sigmoid(proj(P_CFB + p))

    def store_convc(rows, lanes, acc):
        convc_s[rows, lanes] = acc

    conv_blocks = iter(range(0, D_MODEL, D_MODEL // CONV_SPLIT))

    def conv_c_block(n=1):
        for _ in range(n):
            lane0 = next(conv_blocks)
            _causal_dwconv(ubuf_s, cfw_ref, CF_KERNEL, store_convc, lane0, lane0 + D_MODEL // CONV_SPLIT)

    conv_c_block(2)

    cos = cos_ref[...]
    sin = sin_ref[...]
    q = proj(P_Q)
    k = proj(P_K)
    for hd in range(RET_HEADS):
        dk = slice(hd * RET_QK_DIM, (hd + 1) * RET_QK_DIM)
        qh = q[:, dk]
        kh = k[:, dk]
        kr = (kh * cos + pltpu.roll(kh, RET_QK_DIM // 2, axis=1) * sin) * (RET_QK_DIM ** -0.5)
        q_s[:, dk] = (qh * cos + pltpu.roll(qh, RET_QK_DIM // 2, axis=1) * sin).astype(bf16)
        for c in range(MIX_TS // RET_L):
            rows = slice(c * RET_L, (c + 1) * RET_L)
            kt_s[c, dk, :] = kr[rows].T.astype(bf16)
            kz_s[rows, dk] = (kr[rows] * zeta_ref[hd]).astype(bf16)
    conv_c_block()
    for p in range(RET_V // SLAB):
        v_s[:, _panel(p)] = proj(P_V + p).astype(bf16)
    conv_c_block()

    for p in range(D_PANELS):
        zbuf_s[HALO:HALO + MIX_TS, _panel(p)] = proj(P_SCC + p) * proj(P_SCX + p)
    sc_b = jnp.concatenate([proj(P_SCB + p) for p in range(D_PANELS)], axis=1)

    def store_gated_convb(rows, lanes, acc):
        lhs_s[rows, lanes] = (sc_b[rows, lanes] * acc).astype(bf16)

    _causal_dwconv(zbuf_s, scw_ref, SC_KERNEL, store_gated_convb)
    zbuf_s[0:HALO, :] = zbuf_s[MIX_TS:MIX_TS + HALO, :]
    gated_branch(P_GATE + D_PANELS, wso_ref, True)

    for gp in range(RET_V // SLAB):
        g = proj(P_G + gp)
        for i in range(HEADS_PER_PANEL):
            retention_head(gp * HEADS_PER_PANEL + i, g[:, i * RET_V_DIM:(i + 1) * RET_V_DIM])
            conv_c_block()
    ubuf_s[0:HALO, :] = ubuf_s[MIX_TS:MIX_TS + HALO, :]
    gated_branch(P_GATE, wro_ref, False)

    c = convc_s[...] + cfb_ref[...]
    mu = jnp.mean(c, axis=-1, keepdims=True)
    dev = c - mu
    var = jnp.mean(dev * dev, axis=-1, keepdims=True)
    lhs_s[...] = _silu((dev * lax.rsqrt(var + LN_EPS)) * lng_ref[...] + lnb_ref[...]).astype(bf16)
    gated_branch(P_GATE + 2 * D_PANELS, wco_ref, False)

    lhs_s[...] = m_s[...].astype(bf16)
    ys = [jnp.dot(lhs_s[...], wo_ref[p], preferred_element_type=f32) for p in range(D_PANELS)]
    _norm_residual(ys, x_ref, gpost_ref, o_ref, slice(0, MIX_TS), 1.0)


def _decay_tables():
    log_g = jnp.log(1.0 - 2.0 ** (-5.0 - jnp.arange(RET_HEADS, dtype=f32)))
    idx = jnp.arange(RET_L, dtype=f32)
    dist = jnp.abs(idx[:, None] - idx[None, :])
    chunk_id = jnp.arange(RET_L) // CHUNK
    visible = (chunk_id[None, :] <= chunk_id[:, None]).astype(f32)
    dmask = jnp.exp(log_g[:, None, None] * dist[None]) * visible[None]
    xi = jnp.exp(log_g[:, None] * (idx[None, :] + 1.0))
    zeta = jnp.exp(log_g[:, None] * (RET_L - 1.0 - idx[None, :]))
    g_l = jnp.exp(log_g * RET_L)
    xi = jnp.broadcast_to(xi[:, :, None], (RET_HEADS, RET_L, RET_QK_DIM))
    zeta = jnp.broadcast_to(zeta[:, :, None], (RET_HEADS, RET_L, RET_QK_DIM))
    g_l = jnp.broadcast_to(g_l[:, None, None], (RET_HEADS, 1, RET_V_DIM))
    return dmask, xi, zeta, g_l


def _mixer(x, cos_t, sin_t, g_pre, g_post, w_in, w_ret_o, w_sc_o, w_cf_o, w_o,
           sc_conv_w, cf_dw_w, cf_dw_b, cf_ln_g, cf_ln_b, tables, casts=()):
    b, s, d = x.shape
    dmask, xi, zeta, g_l = tables
    tiles = s // MIX_TS
    tile = lambda width: pl.BlockSpec((None, MIX_TS, width), lambda i, j: (i, j, 0))
    row = lambda v: v[None, :]
    rows = HALO + MIX_TS
    cast_in, cast_out, cast_shapes = _cast_specs(casts, b * tiles, lambda i, j: i * tiles + j)
    out = pl.pallas_call(
        functools.partial(_mixer_body, len(casts)),
        grid=(b, tiles),
        in_specs=[tile(d), tile(RET_QK_DIM), tile(RET_QK_DIM),
                  _resident((1, d)), _resident((1, d)),
                  _resident((IN_PANELS, d, SLAB)), _resident((D_PANELS, RET_V, SLAB)),
                  _resident((D_PANELS, d, SLAB)), _resident((D_PANELS, d, SLAB)),
                  _resident((D_PANELS, d, SLAB)),
                  _resident((SC_KERNEL, d)), _resident((CF_KERNEL, d)),
                  _resident((1, d)), _resident((1, d)), _resident((1, d)),
                  _resident(dmask.shape), _resident(xi.shape), _resident(zeta.shape), _resident(g_l.shape)] + cast_in,
        out_specs=[tile(d)] + cast_out,
        out_shape=[jax.ShapeDtypeStruct((b, s, d), f32)] + cast_shapes,
        scratch_shapes=[pltpu.VMEM((MIX_TS, d), bf16),
                        pltpu.VMEM((MIX_TS, RET_QK), bf16),
                        pltpu.VMEM((MIX_TS // RET_L, RET_QK, RET_L), bf16),
                        pltpu.VMEM((MIX_TS, RET_QK), bf16),
                        pltpu.VMEM((MIX_TS, RET_V), bf16),
                        pltpu.VMEM((MIX_TS, d), bf16),
                        pltpu.VMEM((RET_HEADS, RET_QK_DIM, RET_V_DIM), f32),
                        pltpu.VMEM((rows, d), f32),
                        pltpu.VMEM((rows, d), f32),
                        pltpu.VMEM((MIX_TS, d), f32),
                        pltpu.VMEM((MIX_TS, d), f32)],
        compiler_params=pltpu.CompilerParams(
            dimension_semantics=("arbitrary", "arbitrary"),
            vmem_limit_bytes=V7X_VMEM_LIMIT_BYTES),
        name="mixer",
    )(x, cos_t, sin_t, row(g_pre), row(g_post), w_in, w_ret_o, w_sc_o, w_cf_o, w_o,
      sc_conv_w, cf_dw_w, row(cf_dw_b), row(cf_ln_g), row(cf_ln_b), dmask, xi, zeta, g_l, *[job.w for job in casts])
    return out[0], out[1:]


def kernel(x, positions, norm_g, ffn1_w_gu, ffn1_w_down, w_in, w_ret_o, sc_conv_w, w_sc_o, cf_dw_w, cf_dw_b,
           cf_ln_g, cf_ln_b, w_cf_o, w_o, ffn2_w_gu, ffn2_w_down):
    b, s, d = x.shape
    assert d == D_MODEL and s % MIX_TS == 0 and (b * s) % FFN_TM == 0 and MIX_TS % RET_L == 0 and RET_L % CHUNK == 0
    cos_t, sin_t = _rope_tables(positions)
    tables = _decay_tables()
    ffn_w = (ffn1_w_gu[0].astype(bf16), ffn1_w_down[0].astype(bf16))
    for l in range(DEPTH):
        g = norm_g[l]
        mixer_casts = [_Cast(w, l, True) for w in (w_in, w_ret_o, w_sc_o, w_cf_o, w_o)]
        x2, mixer_w = _ffn(x.reshape(b * s, d), g[0], g[1], *ffn_w, mixer_casts)
        x, ffn_w = _mixer(x2.reshape(b, s, d), cos_t, sin_t, g[2], g[3], *mixer_w,
                          sc_conv_w[l], cf_dw_w[l], cf_dw_b[l], cf_ln_g[l], cf_ln_b[l], tables,
                          [_Cast(ffn2_w_gu, l, False), _Cast(ffn2_w_down, l, False)])
        next_casts = [_Cast(ffn1_w_gu, l + 1, False), _Cast(ffn1_w_down, l + 1, False)] if l + 1 < DEPTH else []
        x2, ffn_w = _ffn(x.reshape(b * s, d), g[4], g[5], *ffn_w, next_casts)
        x = x2.reshape(b, s, d)
    return x
```

```python
import functools
from typing import NamedTuple

import jax
import jax.numpy as jnp
from jax import lax
from jax.experimental import pallas as pl
from jax.experimental.pallas import tpu as pltpu

f32 = jnp.float32
bf16 = jnp.bfloat16

D_MODEL = 1024
DEPTH = 2
CHUNK = 64
RET_HEADS = 4
RET_QK_DIM = 128
RET_V_DIM = 256
RET_QK = RET_HEADS * RET_QK_DIM
RET_V = RET_HEADS * RET_V_DIM
SC_KERNEL = 3
CF_KERNEL = 31
D_FF = 4 * D_MODEL
ROPE_BASE = 10000.0
NORM_EPS = 1e-6
LN_EPS = 1e-5

V7X_VMEM_LIMIT_BYTES = 60 * 1024 * 1024
SUBLANES = 8
BF16_ROWS = 16
SLAB = 512
FFN_TM = 512
FFN_PARTS = 2
MIX_TS = 512
RET_L = 256
HALO = 32
CONV_ROWS = 64
CONV_LANES = 128
CONV_SPLIT = 8
ROPE_TS = 512

D_PANELS = D_MODEL // SLAB
P_Q = 0
P_K = P_Q + RET_QK // SLAB
P_V = P_K + RET_QK // SLAB
P_G = P_V + RET_V // SLAB
P_SCB = P_G + RET_V // SLAB
P_SCC = P_SCB + D_PANELS
P_SCX = P_SCC + D_PANELS
P_CFA = P_SCX + D_PANELS
P_CFB = P_CFA + D_PANELS
P_GATE = P_CFB + D_PANELS
IN_PANELS = P_GATE + 3 * D_PANELS
FF_PANELS = D_FF // SLAB
HEADS_PER_PANEL = SLAB // RET_V_DIM


def _silu(x):
    return x * jax.nn.sigmoid(x)


def _panel(p):
    return slice(p * SLAB, (p + 1) * SLAB)


def _resident(shape):
    idx = (0,) * len(shape)
    return pl.BlockSpec(shape, lambda i, j: idx, pipeline_mode=pl.Buffered(1))


class _Cast(NamedTuple):
    w: jax.Array
    layer: int
    panels: bool


def _cast_specs(jobs, steps, step_of):
    in_specs, out_specs, out_shapes = [], [], []
    for job in jobs:
        _, k, n = job.w.shape
        rows = k // steps
        assert rows * steps == k and rows % BF16_ROWS == 0 and n % SLAB == 0
        in_specs.append(pl.BlockSpec((None, rows, n), lambda i, j, layer=job.layer: (layer, step_of(i, j), 0)))
        if job.panels:
            out_specs.append(pl.BlockSpec((n // SLAB, rows, SLAB), lambda i, j: (0, step_of(i, j), 0)))
            out_shapes.append(jax.ShapeDtypeStruct((n // SLAB, k, SLAB), bf16))
        else:
            out_specs.append(pl.BlockSpec((rows, n), lambda i, j: (step_of(i, j), 0)))
            out_shapes.append(jax.ShapeDtypeStruct((k, n), bf16))
    return in_specs, out_specs, out_shapes


def _run_casts(src_refs, dst_refs):
    for src, dst in zip(src_refs, dst_refs):
        if len(dst.shape) == 3:
            for p in range(dst.shape[0]):
                dst[p] = src[:, _panel(p)].astype(bf16)
        else:
            dst[...] = src[...].astype(bf16)


def _rope_body(pos_ref, inv_ref, sign_ref, cos_ref, sin_ref):
    half_rows = ROPE_TS // 2
    half = RET_QK_DIM // 2
    low = lax.broadcasted_iota(jnp.int32, (half_rows, RET_QK_DIM), 1) < half
    pos = jnp.where(low, pos_ref[0:half_rows, :], pos_ref[half_rows:ROPE_TS, :]).astype(f32)
    ang = pos * inv_ref[...]
    c = jnp.cos(ang)
    s = jnp.sin(ang)
    c_swapped = pltpu.roll(c, half, axis=1)
    s_swapped = pltpu.roll(s, half, axis=1)
    cos_ref[0:half_rows, :] = jnp.where(low, c, c_swapped)
    cos_ref[half_rows:ROPE_TS, :] = jnp.where(low, c_swapped, c)
    sin_ref[0:half_rows, :] = jnp.where(low, s, s_swapped) * sign_ref[...]
    sin_ref[half_rows:ROPE_TS, :] = jnp.where(low, s_swapped, s) * sign_ref[...]


def _rope_tables(positions):
    b, s = positions.shape
    half = RET_QK_DIM // 2
    inv = ROPE_BASE ** (-jnp.arange(half, dtype=f32) / half)
    inv2 = jnp.concatenate([inv, inv])[None, :]
    sign = jnp.concatenate([-jnp.ones((half,), f32), jnp.ones((half,), f32)])[None, :]
    out = jax.ShapeDtypeStruct((b, s, RET_QK_DIM), f32)
    return pl.pallas_call(
        _rope_body,
        grid=(b, s // ROPE_TS),
        in_specs=[pl.BlockSpec((None, ROPE_TS, 1), lambda i, j: (i, j, 0)),
                  pl.BlockSpec((1, RET_QK_DIM), lambda i, j: (0, 0)),
                  pl.BlockSpec((1, RET_QK_DIM), lambda i, j: (0, 0))],
        out_specs=[pl.BlockSpec((None, ROPE_TS, RET_QK_DIM), lambda i, j: (i, j, 0))] * 2,
        out_shape=[out, out],
        name="rope_tables",
    )(positions[:, :, None], inv2, sign)


def _rms_scale(x, g):
    ms = jnp.mean(x * x, axis=-1, keepdims=True)
    return (x * lax.rsqrt(ms + NORM_EPS)) * g


def _norm_residual(ys, x_ref, g_ref, o_ref, rows, scale):
    ms = sum(jnp.sum(y * y, axis=-1, keepdims=True) for y in ys) * (1.0 / D_MODEL)
    inv = lax.rsqrt(ms + NORM_EPS)
    for p, y in enumerate(ys):
        o_ref[rows, _panel(p)] = x_ref[rows, _panel(p)] + scale * ((y * inv) * g_ref[:, _panel(p)])


def _ffn_body(n_cast, x_ref, gpre_ref, gpost_ref, wgu_ref, wd_ref, *refs):
    cast_src, o_ref, cast_dst, (h_s, a_s) = refs[:n_cast], refs[n_cast], refs[n_cast + 1:2 * n_cast + 1], refs[2 * n_cast + 1:]
    _run_casts(cast_src, cast_dst)
    part = FFN_TM // FFN_PARTS
    parts = [slice(i * part, (i + 1) * part) for i in range(FFN_PARTS)]
    for rows in parts:
        h_s[rows, :] = _rms_scale(x_ref[rows, :], gpre_ref[...]).astype(bf16)
    for p in range(FF_PANELS):
        for rows in parts:
            g = jnp.dot(h_s[rows, :], wgu_ref[:, _panel(p)], preferred_element_type=f32)
            u = jnp.dot(h_s[rows, :], wgu_ref[:, _panel(FF_PANELS + p)], preferred_element_type=f32)
            a_s[rows, _panel(p)] = (_silu(g) * u).astype(bf16)
    for rows in parts:
        ys = [jnp.dot(a_s[rows, :], wd_ref[:, _panel(p)], preferred_element_type=f32) for p in range(D_PANELS)]
        _norm_residual(ys, x_ref, gpost_ref, o_ref, rows, 0.5)


def _ffn(x2, g_pre, g_post, w_gu, w_down, casts=()):
    t, d = x2.shape
    steps = t // FFN_TM
    tile = pl.BlockSpec((FFN_TM, d), lambda i, j: (i, 0))
    cast_in, cast_out, cast_shapes = _cast_specs(casts, steps, lambda i, j: i)
    out = pl.pallas_call(
        functools.partial(_ffn_body, len(casts)),
        grid=(steps, 1),
        in_specs=[tile, _resident((1, d)), _resident((1, d)),
                  _resident((d, 2 * D_FF)), _resident((D_FF, d))] + cast_in,
        out_specs=[tile] + cast_out,
        out_shape=[jax.ShapeDtypeStruct((t, d), f32)] + cast_shapes,
        scratch_shapes=[pltpu.VMEM((FFN_TM, d), bf16),
                        pltpu.VMEM((FFN_TM, D_FF), bf16)],
        compiler_params=pltpu.CompilerParams(
            dimension_semantics=("arbitrary", "arbitrary"),
            vmem_limit_bytes=V7X_VMEM_LIMIT_BYTES),
        name="ffn",
    )(x2, g_pre[None, :], g_post[None, :], w_gu, w_down, *[job.w for job in casts])
    return out[0], out[1:]


def _causal_dwconv(buf_ref, w_ref, ksize, emit, lane0=0, lane1=D_MODEL):
    halo = SUBLANES * pl.cdiv(ksize - 1, SUBLANES)
    for c0 in range(lane0, lane1, CONV_LANES):
        lanes = slice(c0, c0 + CONV_LANES)
        for r0 in range(0, MIX_TS, CONV_ROWS):
            win = buf_ref[HALO - halo + r0:HALO + r0 + CONV_ROWS, lanes]
            acc = None
            for r in range(min(SUBLANES, ksize)):
                rolled = win if r == 0 else pltpu.roll(win, r, axis=0)
                for a in range(halo // SUBLANES):
                    s = SUBLANES * a + r
                    if s > ksize - 1:
                        continue
                    start = halo - SUBLANES * a
                    term = w_ref[ksize - 1 - s:ksize - s, lanes] * rolled[start:start + CONV_ROWS]
                    acc = term if acc is None else acc + term
            emit(slice(r0, r0 + CONV_ROWS), lanes, acc)


def _mixer_body(n_cast, x_ref, cos_ref, sin_ref, gpre_ref, gpost_ref, win_ref, wro_ref, wso_ref, wco_ref, wo_ref,
                scw_ref, cfw_ref, cfb_ref, lng_ref, lnb_ref, dmask_ref, xi_ref, zeta_ref, gl_ref, *refs):
    cast_src, o_ref, cast_dst = refs[:n_cast], refs[n_cast], refs[n_cast + 1:2 * n_cast + 1]
    h_s, q_s, kt_s, kz_s, v_s, lhs_s, state_s, zbuf_s, ubuf_s, convc_s, m_s = refs[2 * n_cast + 1:]
    _run_casts(cast_src, cast_dst)
    t = pl.program_id(1)

    @pl.when(t == 0)
    def _():
        state_s[...] = jnp.zeros_like(state_s)
        zbuf_s[0:HALO, :] = jnp.zeros((HALO, D_MODEL), f32)
        ubuf_s[0:HALO, :] = jnp.zeros((HALO, D_MODEL), f32)

    h_s[...] = _rms_scale(x_ref[...], gpre_ref[...]).astype(bf16)

    def proj(panel):
        return jnp.dot(h_s[...], win_ref[panel], preferred_element_type=f32)

    def retention_head(hd, g_head):
        dk = slice(hd * RET_QK_DIM, (hd + 1) * RET_QK_DIM)
        dv = slice(hd * RET_V_DIM, (hd + 1) * RET_V_DIM)
        xi = jnp.concatenate([xi_ref[hd]] * (RET_V_DIM // RET_QK_DIM), axis=1)
        for c in range(MIX_TS // RET_L):
            rows = slice(c * RET_L, (c + 1) * RET_L)
            qb = q_s[rows, dk]
            vh = v_s[rows, dv]
            scores = jnp.dot(qb, kt_s[c, dk, :], preferred_element_type=f32) * dmask_ref[hd]
            state = state_s[hd]
            o = (jnp.dot(scores.astype(bf16), vh, preferred_element_type=f32)
                 + jnp.dot(qb, state.astype(bf16), preferred_element_type=f32) * xi)
            state_s[hd] = state * gl_ref[hd] + lax.dot_general(kz_s[rows, dk], vh, (((0,), (0,)), ((), ())),
                                                               preferred_element_type=f32)
            mu = jnp.mean(o, axis=-1, keepdims=True)
            dev = o - mu
            var = jnp.mean(dev * dev, axis=-1, keepdims=True)
            lhs_s[rows, dv] = (_silu(g_head[rows]) * (dev * lax.rsqrt(var + LN_EPS))).astype(bf16)

    def gated_branch(panel0, w_ref, first):
        for p in range(D_PANELS):
            y = jax.nn.sigmoid(proj(panel0 + p)) * jnp.dot(lhs_s[...], w_ref[p], preferred_element_type=f32)
            m_s[:, _panel(p)] = y if first else m_s[:, _panel(p)] + y

    for p in range(D_PANELS):
        ubuf_s[HALO:HALO + MIX_TS, _panel(p)] = proj(P_CFA + p) * jax.nn.sigmoid(proj(P_CFB + p))

    def store_convc(rows, lanes, acc):
        convc_s[rows, lanes] = acc

    conv_blocks = iter(range(0, D_MODEL, D_MODEL // CONV_SPLIT))

    def conv_c_block(n=1):
        for _ in range(n):
            lane0 = next(conv_blocks)
            _causal_dwconv(ubuf_s, cfw_ref, CF_KERNEL, store_convc, lane0, lane0 + D_MODEL // CONV_SPLIT)

    conv_c_block(2)

    cos = cos_ref[...]
    sin = sin_ref[...]
    q = proj(P_Q)
    k = proj(P_K)
    for hd in range(RET_HEADS):
        dk = slice(hd * RET_QK_DIM, (hd + 1) * RET_QK_DIM)
        qh = q[:, dk]
        kh = k[:, dk]
        kr = (kh * cos + pltpu.roll(kh, RET_QK_DIM // 2, axis=1) * sin) * (RET_QK_DIM ** -0.5)
        q_s[:, dk] = (qh * cos + pltpu.roll(qh, RET_QK_DIM // 2, axis=1) * sin).astype(bf16)
        for c in range(MIX_TS // RET_L):
            rows = slice(c * RET_L, (c + 1) * RET_L)
            kt_s[c, dk, :] = kr[rows].T.astype(bf16)
            kz_s[rows, dk] = (kr[rows] * zeta_ref[hd]).astype(bf16)
    conv_c_block()
    for p in range(RET_V // SLAB):
        v_s[:, _panel(p)] = proj(P_V + p).astype(bf16)
    conv_c_block()

    for p in range(D_PANELS):
        zbuf_s[HALO:HALO + MIX_TS, _panel(p)] = proj(P_SCC + p) * proj(P_SCX + p)
    sc_b = jnp.concatenate([proj(P_SCB + p) for p in range(D_PANELS)], axis=1)

    def store_gated_convb(rows, lanes, acc):
        lhs_s[rows, lanes] = (sc_b[rows, lanes] * acc).astype(bf16)

    _causal_dwconv(zbuf_s, scw_ref, SC_KERNEL, store_gated_convb)
    zbuf_s[0:HALO, :] = zbuf_s[MIX_TS:MIX_TS + HALO, :]
    gated_branch(P_GATE + D_PANELS, wso_ref, True)

    for gp in range(RET_V // SLAB):
        g = proj(P_G + gp)
        for i in range(HEADS_PER_PANEL):
            retention_head(gp * HEADS_PER_PANEL + i, g[:, i * RET_V_DIM:(i + 1) * RET_V_DIM])
            conv_c_block()
    ubuf_s[0:HALO, :] = ubuf_s[MIX_TS:MIX_TS + HALO, :]
    gated_branch(P_GATE, wro_ref, False)

    c = convc_s[...] + cfb_ref[...]
    mu = jnp.mean(c, axis=-1, keepdims=True)
    dev = c - mu
    var = jnp.mean(dev * dev, axis=-1, keepdims=True)
    lhs_s[...] = _silu((dev * lax.rsqrt(var + LN_EPS)) * lng_ref[...] + lnb_ref[...]).astype(bf16)
    gated_branch(P_GATE + 2 * D_PANELS, wco_ref, False)

    lhs_s[...] = m_s[...].astype(bf16)
    ys = [jnp.dot(lhs_s[...], wo_ref[p], preferred_element_type=f32) for p in range(D_PANELS)]
    _norm_residual(ys, x_ref, gpost_ref, o_ref, slice(0, MIX_TS), 1.0)


def _decay_tables():
    log_g = jnp.log(1.0 - 2.0 ** (-5.0 - jnp.arange(RET_HEADS, dtype=f32)))
    idx = jnp.arange(RET_L, dtype=f32)
    dist = jnp.abs(idx[:, None] - idx[None, :])
    chunk_id = jnp.arange(RET_L) // CHUNK
    visible = (chunk_id[None, :] <= chunk_id[:, None]).astype(f32)
    dmask = jnp.exp(log_g[:, None, None] * dist[None]) * visible[None]
    xi = jnp.exp(log_g[:, None] * (idx[None, :] + 1.0))
    zeta = jnp.exp(log_g[:, None] * (RET_L - 1.0 - idx[None, :]))
    g_l = jnp.exp(log_g * RET_L)
    xi = jnp.broadcast_to(xi[:, :, None], (RET_HEADS, RET_L, RET_QK_DIM))
    zeta = jnp.broadcast_to(zeta[:, :, None], (RET_HEADS, RET_L, RET_QK_DIM))
    g_l = jnp.broadcast_to(g_l[:, None, None], (RET_HEADS, 1, RET_V_DIM))
    return dmask, xi, zeta, g_l


def _mixer(x, cos_t, sin_t, g_pre, g_post, w_in, w_ret_o, w_sc_o, w_cf_o, w_o,
           sc_conv_w, cf_dw_w, cf_dw_b, cf_ln_g, cf_ln_b, tables, casts=()):
    b, s, d = x.shape
    dmask, xi, zeta, g_l = tables
    tiles = s // MIX_TS
    tile = lambda width: pl.BlockSpec((None, MIX_TS, width), lambda i, j: (i, j, 0))
    row = lambda v: v[None, :]
    rows = HALO + MIX_TS
    cast_in, cast_out, cast_shapes = _cast_specs(casts, b * tiles, lambda i, j: i * tiles + j)
    out = pl.pallas_call(
        functools.partial(_mixer_body, len(casts)),
        grid=(b, tiles),
        in_specs=[tile(d), tile(RET_QK_DIM), tile(RET_QK_DIM),
                  _resident((1, d)), _resident((1, d)),
                  _resident((IN_PANELS, d, SLAB)), _resident((D_PANELS, RET_V, SLAB)),
                  _resident((D_PANELS, d, SLAB)), _resident((D_PANELS, d, SLAB)),
                  _resident((D_PANELS, d, SLAB)),
                  _resident((SC_KERNEL, d)), _resident((CF_KERNEL, d)),
                  _resident((1, d)), _resident((1, d)), _resident((1, d)),
                  _resident(dmask.shape), _resident(xi.shape), _resident(zeta.shape), _resident(g_l.shape)] + cast_in,
        out_specs=[tile(d)] + cast_out,
        out_shape=[jax.ShapeDtypeStruct((b, s, d), f32)] + cast_shapes,
        scratch_shapes=[pltpu.VMEM((MIX_TS, d), bf16),
                        pltpu.VMEM((MIX_TS, RET_QK), bf16),
                        pltpu.VMEM((MIX_TS // RET_L, RET_QK, RET_L), bf16),
                        pltpu.VMEM((MIX_TS, RET_QK), bf16),
                        pltpu.VMEM((MIX_TS, RET_V), bf16),
                        pltpu.VMEM((MIX_TS, d), bf16),
                        pltpu.VMEM((RET_HEADS, RET_QK_DIM, RET_V_DIM), f32),
                        pltpu.VMEM((rows, d), f32),
                        pltpu.VMEM((rows, d), f32),
                        pltpu.VMEM((MIX_TS, d), f32),
                        pltpu.VMEM((MIX_TS, d), f32)],
        compiler_params=pltpu.CompilerParams(
            dimension_semantics=("arbitrary", "arbitrary"),
            vmem_limit_bytes=V7X_VMEM_LIMIT_BYTES),
        name="mixer",
    )(x, cos_t, sin_t, row(g_pre), row(g_post), w_in, w_ret_o, w_sc_o, w_cf_o, w_o,
      sc_conv_w, cf_dw_w, row(cf_dw_b), row(cf_ln_g), row(cf_ln_b), dmask, xi, zeta, g_l, *[job.w for job in casts])
    return out[0], out[1:]


def kernel(x, positions, norm_g, ffn1_w_gu, ffn1_w_down, w_in, w_ret_o, sc_conv_w, w_sc_o, cf_dw_w, cf_dw_b,
           cf_ln_g, cf_ln_b, w_cf_o, w_o, ffn2_w_gu, ffn2_w_down):
    b, s, d = x.shape
    assert d == D_MODEL and s % MIX_TS == 0 and (b * s) % FFN_TM == 0 and MIX_TS % RET_L == 0 and RET_L % CHUNK == 0
    cos_t, sin_t = _rope_tables(positions)
    tables = _decay_tables()
    ffn_w = (ffn1_w_gu[0].astype(bf16), ffn1_w_down[0].astype(bf16))
    for l in range(DEPTH):
        g = norm_g[l]
        mixer_casts = [_Cast(w, l, True) for w in (w_in, w_ret_o, w_sc_o, w_cf_o, w_o)]
        x2, mixer_w = _ffn(x.reshape(b * s, d), g[0], g[1], *ffn_w, mixer_casts)
        x, ffn_w = _mixer(x2.reshape(b, s, d), cos_t, sin_t, g[2], g[3], *mixer_w,
                          sc_conv_w[l], cf_dw_w[l], cf_dw_b[l], cf_ln_g[l], cf_ln_b[l], tables,
                          [_Cast(ffn2_w_gu, l, False), _Cast(ffn2_w_down, l, False)])
        next_casts = [_Cast(ffn1_w_gu, l + 1, False), _Cast(ffn1_w_down, l + 1, False)] if l + 1 < DEPTH else []
        x2, ffn_w = _ffn(x.reshape(b * s, d), g[4], g[5], *ffn_w, next_casts)
        x = x2.reshape(b, s, d)
    return x
```
